```python
import math
import jax, jax.numpy as jnp
from jax import lax
import numpy as np

D_MODEL = 1024
BATCH = 8
SEQ = 8192
DEPTH = 1

EPS = 1e-6

DA_HEADS = 8
DA_HEAD_DIM = 64
DA_V_HEAD_DIM = 2 * DA_HEAD_DIM
DA_QK_WIDTH = DA_HEADS * 2 * DA_HEAD_DIM
DA_V_WIDTH = DA_HEADS * DA_V_HEAD_DIM
Q_BLOCK = 128

RET_HEADS = 4
RET_KEY_DIM = D_MODEL // RET_HEADS
RET_VAL_DIM = 2 * RET_KEY_DIM
RET_QK_WIDTH = RET_HEADS * RET_KEY_DIM
RET_V_WIDTH = RET_HEADS * RET_VAL_DIM
RET_CHUNK = 128
RET_ROT_BASE = 10000.0

DA_Q_OFF = 0
DA_K_OFF = DA_Q_OFF + DA_QK_WIDTH
DA_V_OFF = DA_K_OFF + DA_QK_WIDTH
RET_Q_OFF = DA_V_OFF + DA_V_WIDTH
RET_K_OFF = RET_Q_OFF + RET_QK_WIDTH
RET_V_OFF = RET_K_OFF + RET_QK_WIDTH
RET_G_OFF = RET_V_OFF + RET_V_WIDTH
GATE_A_OFF = RET_G_OFF + RET_V_WIDTH
GATE_B_OFF = GATE_A_OFF + D_MODEL
IN_WIDTH = GATE_B_OFF + D_MODEL

N_GROUPS = 4
EXPERTS_PER_GROUP = 8
N_EXPERTS = N_GROUPS * EXPERTS_PER_GROUP
TOP_K_IN_GROUP = 2
D_FF_EXPERT = D_MODEL // 2

kernel_name = "hybrid_diffattn_retention_hiermoe"


def rmsnorm(t, w):
    t32 = t.astype(jnp.float32)
    r = lax.rsqrt(jnp.mean(t32 * t32, axis=-1, keepdims=True) + EPS)
    return (t32 * r).astype(t.dtype) * w


def proj_cols(h, w, lo, hi):
    return h @ w[:, lo:hi]


def diff_attention(q, k, v, lam, lam_init, subln_w):
    S = q.shape[3]
    scale = DA_HEAD_DIM ** -0.5
    outs = []
    for i in range(S // Q_BLOCK):
        lo = i * Q_BLOCK
        hi = lo + Q_BLOCK
        qb = q[:, :, :, lo:hi]
        kb = k[:, :, :, :hi]
        vb = v[:, :, :hi]
        s = jnp.einsum('bhmqd,bhmkd->bhmqk', qb, kb).astype(jnp.float32) * scale
        causal = jnp.arange(hi)[None, :] <= (lo + jnp.arange(Q_BLOCK))[:, None]
        p = jax.nn.softmax(jnp.where(causal, s, -jnp.inf), axis=-1)
        a = p[:, :, 0] - lam * p[:, :, 1]
        outs.append(jnp.einsum('bhqk,bhkv->bhqv', a.astype(vb.dtype), vb))
    o = jnp.concatenate(outs, axis=2)
    return rmsnorm(o, subln_w) * (1.0 - lam_init)


def retnet_rotate(t, pos):
    d = t.shape[-1]
    angle = 1.0 / (RET_ROT_BASE ** jnp.linspace(0.0, 1.0, d // 2, dtype=jnp.float32))
    angle = jnp.repeat(angle, 2)
    ang = pos.astype(jnp.float32)[:, None] * angle[None, :]
    sin = jnp.sin(ang)[:, None, :]
    cos = jnp.cos(ang)[:, None, :]
    t1 = t[..., 0::2]
    t2 = t[..., 1::2]
    rot = jnp.stack((-t2, t1), axis=-1).reshape(t.shape)
    return t * cos + rot * sin


def retention_chunkwise(q, k, v):
    B, H, S, dk = q.shape
    dv = v.shape[-1]
    C = RET_CHUNK
    N = S // C
    log_g = jnp.log1p(-jnp.exp2(-5.0 - jnp.arange(H, dtype=jnp.float32)))
    j = jnp.arange(C, dtype=jnp.float32)
    rel = j[:, None] - j[None, :]
    inner_decay = jnp.where(rel >= 0, jnp.exp(log_g[:, None, None] * jnp.maximum(rel, 0.0)), 0.0)
    xi = jnp.exp(log_g[:, None] * (j + 1.0))
    zeta = jnp.exp(log_g[:, None] * (C - 1.0 - j))
    chunk_decay = jnp.exp(log_g * C)

    def to_chunks(t):
        return jnp.moveaxis(t.reshape(B, H, N, C, t.shape[-1]), 2, 0)

    def step(state, qkv):
        qc, kc, vc = qkv
        inner = jnp.einsum('bhqd,bhkd->bhqk', qc, kc) * inner_decay
        o = (jnp.einsum('bhqk,bhkv->bhqv', inner, vc)
             + jnp.einsum('bhqd,bhdv->bhqv', qc * xi[..., None], state))
        state = (chunk_decay[:, None, None] * state
                 + jnp.einsum('bhkd,bhkv->bhdv', kc * zeta[..., None], vc))
        return state, o

    state0 = jnp.zeros((B, H, dk, dv), jnp.float32)
    _, o = lax.scan(step, state0, (to_chunks(q), to_chunks(k), to_chunks(v)))
    return jnp.moveaxis(o, 0, 2).reshape(B, H, S, dv)


def hier_moe(xn, w_gr, b_gr, w_er, b_er, w1, w3, w2):
    T = xn.shape[0]
    p_group = jax.nn.softmax((xn @ w_gr).astype(jnp.float32) + b_gr, axis=-1)
    g_idx = jnp.argmax(p_group, axis=-1)
    p_g = jnp.take_along_axis(p_group, g_idx[:, None], axis=-1)
    e_logits = (xn @ w_er).astype(jnp.float32).reshape(T, N_GROUPS, EXPERTS_PER_GROUP) + b_er
    e_logits = jnp.take_along_axis(e_logits, g_idx[:, None, None], axis=1)[:, 0]
    p_e = jax.nn.softmax(e_logits, axis=-1)
    top_p, top_i = lax.top_k(p_e, TOP_K_IN_GROUP)
    top_w = top_p / jnp.sum(top_p, axis=-1, keepdims=True) * p_g
    expert_id = g_idx[:, None] * EXPERTS_PER_GROUP + top_i
    gates = jnp.sum(jax.nn.one_hot(expert_id, N_EXPERTS, dtype=jnp.float32) * top_w[..., None], axis=1)

    def expert_step(acc, params):
        w1e, w3e, w2e, ge = params
        h = jax.nn.silu(xn @ w1e) * (xn @ w3e)
        return acc + ge[:, None].astype(xn.dtype) * (h @ w2e), None

    out, _ = lax.scan(expert_step, jnp.zeros_like(xn), (w1, w3, w2, gates.T))
    return out


def setup_inputs(seed: int = 0) -> dict:
    key = jax.random.key(seed)
    ks = jax.random.split(key, 24)
    f32 = jnp.float32
    L = DEPTH

    def nrm(k, shape, scale):
        return jax.random.normal(k, shape, f32) * scale

    def gain(k, n):
        return 1.0 + 0.01 * jax.random.normal(k, (L, n), f32)

    return {
        "x": jax.random.normal(ks[0], (BATCH, SEQ, D_MODEL), f32),
        "attn_norm_w": gain(ks[1], D_MODEL),
        "w_in": nrm(ks[2], (L, D_MODEL, IN_WIDTH), D_MODEL ** -0.5),
        "q_norm_w": gain(ks[3], DA_HEAD_DIM),
        "k_norm_w": gain(ks[4], DA_HEAD_DIM),
        "lambda_q1": nrm(ks[5], (L, DA_HEAD_DIM), 0.1),
        "lambda_k1": nrm(ks[6], (L, DA_HEAD_DIM), 0.1),
        "lambda_q2": nrm(ks[7], (L, DA_HEAD_DIM), 0.1),
        "lambda_k2": nrm(ks[8], (L, DA_HEAD_DIM), 0.1),
        "da_subln_w": gain(ks[9], DA_V_HEAD_DIM),
        "ret_norm_w": gain(ks[10], RET_VAL_DIM),
        "w_branch_a": nrm(ks[11], (L, DA_V_WIDTH, D_MODEL), DA_V_WIDTH ** -0.5),
        "w_branch_b": nrm(ks[12], (L, RET_V_WIDTH, D_MODEL), RET_V_WIDTH ** -0.5),
        "w_out": nrm(ks[13], (L, D_MODEL, D_MODEL), D_MODEL ** -0.5),
        "moe_norm_w": gain(ks[14], D_MODEL),
        "w_group_router": nrm(ks[15], (L, D_MODEL, N_GROUPS), D_MODEL ** -0.5),
        "b_group_router": nrm(ks[16], (L, N_GROUPS), 0.01),
        "w_expert_router": nrm(ks[17], (L, D_MODEL, N_EXPERTS), D_MODEL ** -0.5),
        "b_expert_router": nrm(ks[18], (L, N_GROUPS, EXPERTS_PER_GROUP), 0.01),
        "w1": nrm(ks[19], (L, N_EXPERTS, D_MODEL, D_FF_EXPERT), D_MODEL ** -0.5),
        "w3": nrm(ks[20], (L, N_EXPERTS, D_MODEL, D_FF_EXPERT), D_MODEL ** -0.5),
        "w2": nrm(ks[21], (L, N_EXPERTS, D_FF_EXPERT, D_MODEL), D_FF_EXPERT ** -0.5),
    }


def reference(x, attn_norm_w, w_in, q_norm_w, k_norm_w, lambda_q1, lambda_k1, lambda_q2, lambda_k2,
              da_subln_w, ret_norm_w, w_branch_a, w_branch_b, w_out, moe_norm_w,
              w_group_router, b_group_router, w_expert_router, b_expert_router, w1, w3, w2):
    B, S, D = x.shape
    pos = jnp.arange(S)
    for l in range(DEPTH):
        lam_init = 0.8 - 0.6 * math.exp(-0.3 * l)
        h = rmsnorm(x, attn_norm_w[l])
        W = w_in[l]

        qa = proj_cols(h, W, DA_Q_OFF, DA_K_OFF).reshape(B, S, DA_HEADS, 2, DA_HEAD_DIM)
        ka = proj_cols(h, W, DA_K_OFF, DA_V_OFF).reshape(B, S, DA_HEADS, 2, DA_HEAD_DIM)
        va = proj_cols(h, W, DA_V_OFF, RET_Q_OFF).reshape(B, S, DA_HEADS, DA_V_HEAD_DIM)
        qa = jnp.transpose(rmsnorm(qa, q_norm_w[l]), (0, 2, 3, 1, 4))
        ka = jnp.transpose(rmsnorm(ka, k_norm_w[l]), (0, 2, 3, 1, 4))
        va = jnp.transpose(va, (0, 2, 1, 3))
        lam = (jnp.exp(jnp.sum(lambda_q1[l].astype(jnp.float32) * lambda_k1[l].astype(jnp.float32)))
               - jnp.exp(jnp.sum(lambda_q2[l].astype(jnp.float32) * lambda_k2[l].astype(jnp.float32)))
               + lam_init)
        oa = diff_attention(qa, ka, va, lam, lam_init, da_subln_w[l])
        ya = jnp.transpose(oa, (0, 2, 1, 3)).reshape(B, S, DA_V_WIDTH) @ w_branch_a[l]

        qr = proj_cols(h, W, RET_Q_OFF, RET_K_OFF).reshape(B, S, RET_HEADS, RET_KEY_DIM).astype(jnp.float32)
        kr = proj_cols(h, W, RET_K_OFF, RET_V_OFF).reshape(B, S, RET_HEADS, RET_KEY_DIM).astype(jnp.float32)
        vr = proj_cols(h, W, RET_V_OFF, RET_G_OFF).reshape(B, S, RET_HEADS, RET_VAL_DIM).astype(jnp.float32)
        gr = proj_cols(h, W, RET_G_OFF, GATE_A_OFF)
        qr = retnet_rotate(qr, pos)
        kr = retnet_rotate(kr, pos) * (RET_KEY_DIM ** -0.5)
        orr = retention_chunkwise(jnp.transpose(qr, (0, 2, 1, 3)),
                                  jnp.transpose(kr, (0, 2, 1, 3)),
                                  jnp.transpose(vr, (0, 2, 1, 3)))
        orr = rmsnorm(orr, ret_norm_w[l].astype(jnp.float32))
        orr = jnp.transpose(orr, (0, 2, 1, 3)).reshape(B, S, RET_V_WIDTH).astype(x.dtype)
        yb = (jax.nn.silu(gr) * orr) @ w_branch_b[l]

        gate_a = jax.nn.sigmoid(proj_cols(h, W, GATE_A_OFF, GATE_B_OFF))
        gate_b = jax.nn.sigmoid(proj_cols(h, W, GATE_B_OFF, IN_WIDTH))
        x = x + (gate_a * ya + gate_b * yb) @ w_out[l]

        hm = rmsnorm(x, moe_norm_w[l]).reshape(B * S, D)
        x = x + hier_moe(hm, w_group_router[l], b_group_router[l], w_expert_router[l],
                         b_expert_router[l], w1[l], w3[l], w2[l]).reshape(B, S, D)
    return x
```

```python
import functools
import math

import numpy as np
import jax
import jax.numpy as jnp
from jax import lax
from jax.experimental import pallas as pl
from jax.experimental.pallas import tpu as pltpu

F32 = jnp.float32
BF16 = jnp.bfloat16
I32 = jnp.int32

D_MODEL = 1024
EPS = 1e-6

DA_HEADS = 8
DA_HEAD_DIM = 64
DA_V_HEAD_DIM = 128

RET_HEADS = 4
RET_KEY_DIM = 256
RET_VAL_DIM = 512
RET_CHUNK = 128
RET_ROT_BASE = 10000.0

SEG = 1024
IN_WIDTH = 11 * SEG
J_DA_Q, J_DA_K, J_DA_V, J_RET_Q, J_RET_K = 0, 1, 2, 3, 4
J_RET_G0, J_RET_G1, J_GATE_A, J_GATE_B = 7, 8, 9, 10

N_GROUPS = 4
EXPERTS_PER_GROUP = 8
N_EXPERTS = 32
D_FF = 512
PAIRS_PER_GROUP = 28
N_BUCKETS = N_GROUPS * PAIRS_PER_GROUP
ROUTE_LANES = 128

NEG_BIG = -1e30
LOG2E = 1.4426950408889634

VMEM_LIMIT = 56 * 1024 * 1024


def _cparams(sem):
    return pltpu.CompilerParams(dimension_semantics=sem, vmem_limit_bytes=VMEM_LIMIT)


def _proj_kernel(x_ref, nw_ref, w_ref, cos_ref, sin_ref, qnw_ref, knw_ref, gsum_ref, o_ref, h_scr):
    j = pl.program_id(1)
    tm = x_ref.shape[0]

    @pl.when(j == 0)
    def _():
        x = x_ref[...]
        r = lax.rsqrt(jnp.mean(x * x, axis=-1, keepdims=True) + EPS)
        h_scr[...] = ((x * r) * nw_ref[...]).astype(BF16)

    y = jnp.dot(h_scr[...], w_ref[...], preferred_element_type=F32)

    def qk_norm(w_ref_, scale):
        for c in range(SEG // 256):
            sl = slice(c * 256, (c + 1) * 256)
            yc = y[:, sl]
            ms = jnp.dot((yc * yc).astype(BF16), gsum_ref[...], preferred_element_type=F32)
            o_ref[:, sl] = ((yc * lax.rsqrt(ms + EPS)) * w_ref_[:, sl] * scale).astype(o_ref.dtype)

    @pl.when(j == J_DA_Q)
    def _():
        qk_norm(qnw_ref, DA_HEAD_DIM ** -0.5 * LOG2E)

    @pl.when(j == J_DA_K)
    def _():
        qk_norm(knw_ref, 1.0)

    def rotate(scale):
        lane = lax.broadcasted_iota(I32, (tm, RET_KEY_DIM), 1)
        even = (lane & 1) == 0
        cos = cos_ref[...]
        sin = sin_ref[...]
        for c in range(SEG // RET_KEY_DIM):
            sl = slice(c * RET_KEY_DIM, (c + 1) * RET_KEY_DIM)
            t = y[:, sl]
            t_next = pltpu.roll(t, RET_KEY_DIM - 1, 1)
            t_prev = pltpu.roll(t, 1, 1)
            rot = jnp.where(even, -t_next, t_prev)
            o_ref[:, sl] = ((t * cos + rot * sin) * scale).astype(o_ref.dtype)

    @pl.when(j == J_RET_Q)
    def _():
        rotate(1.0)

    @pl.when(j == J_RET_K)
    def _():
        rotate(RET_KEY_DIM ** -0.5)

    @pl.when((j == J_DA_V) | (j == 5) | (j == 6))
    def _():
        o_ref[...] = y.astype(o_ref.dtype)

    @pl.when((j == J_RET_G0) | (j == J_RET_G1))
    def _():
        o_ref[...] = (y * jax.nn.sigmoid(y)).astype(o_ref.dtype)

    @pl.when((j == J_GATE_A) | (j == J_GATE_B))
    def _():
        o_ref[...] = jax.nn.sigmoid(y).astype(o_ref.dtype)


def _proj(x2, attn_norm_w, w_in_bf, cos_t, sin_t, qnw_t, knw_t, gsum, seq, tm):
    T = x2.shape[0]
    n_pos_blocks = seq // tm
    return pl.pallas_call(
        _proj_kernel,
        grid=(T // tm, IN_WIDTH // SEG),
        in_specs=[
            pl.BlockSpec((tm, D_MODEL), lambda i, j: (i, 0)),
            pl.BlockSpec((1, D_MODEL), lambda i, j: (0, 0)),
            pl.BlockSpec((D_MODEL, SEG), lambda i, j: (0, j)),
            pl.BlockSpec((tm, RET_KEY_DIM), lambda i, j: (i % n_pos_blocks, 0)),
            pl.BlockSpec((tm, RET_KEY_DIM), lambda i, j: (i % n_pos_blocks, 0)),
            pl.BlockSpec((1, SEG), lambda i, j: (0, 0)),
            pl.BlockSpec((1, SEG), lambda i, j: (0, 0)),
            pl.BlockSpec((256, 256), lambda i, j: (0, 0)),
        ],
        out_specs=pl.BlockSpec((tm, SEG), lambda i, j: (i, j)),
        out_shape=jax.ShapeDtypeStruct((T, IN_WIDTH), BF16),
        scratch_shapes=[pltpu.VMEM((tm, D_MODEL), BF16)],
        compiler_params=_cparams(("arbitrary", "arbitrary")),
        name="proj",
    )(x2, attn_norm_w, w_in_bf, cos_t, sin_t, qnw_t, knw_t, gsum)


def _attn_kernel(q_ref, k_ref, v_ref, lam_ref, subw_ref, o_ref, q2_scr, m_scr, acc_scr, *, lam_init):
    qi = pl.program_id(2)
    ki = pl.program_id(3)
    tq = q_ref.shape[0]
    tk = k_ref.shape[0]

    @pl.when(ki == 0)
    def _():
        q = q_ref[...]
        lane = lax.broadcasted_iota(I32, q.shape, 1)
        zero = jnp.zeros_like(q)
        q2_scr[0:tq, :] = jnp.where(lane < DA_HEAD_DIM, q, zero)
        q2_scr[tq:2 * tq, :] = jnp.where(lane >= DA_HEAD_DIM, q, zero)
        m_scr[...] = jnp.full(m_scr.shape, NEG_BIG, F32)
        acc_scr[...] = jnp.zeros(acc_scr.shape, F32)

    def step(masked):
        s = lax.dot_general(q2_scr[...], k_ref[...], (((1,), (1,)), ((), ())),
                            preferred_element_type=F32)
        if masked:
            row = lax.broadcasted_iota(I32, (2 * tq, tk), 0)
            row = jnp.where(row >= tq, row - tq, row)
            col = lax.broadcasted_iota(I32, (2 * tq, tk), 1)
            s = jnp.where(col > row, NEG_BIG, s)
        m_prev = m_scr[...]
        m_new = jnp.maximum(m_prev, jnp.max(s, axis=1, keepdims=True))
        alpha = jnp.exp2(m_prev - m_new)
        p = jnp.exp2(s - m_new)
        vext = jnp.concatenate([v_ref[...], jnp.ones((tk, DA_V_HEAD_DIM), BF16)], axis=1)
        acc_scr[...] = alpha * acc_scr[...] + jnp.dot(p.astype(BF16), vext, preferred_element_type=F32)
        m_scr[...] = m_new

    @pl.when(ki < qi)
    def _():
        step(False)

    @pl.when(ki == qi)
    def _():
        step(True)
        acc = acc_scr[...]
        o1 = acc[0:tq, 0:DA_V_HEAD_DIM] / acc[0:tq, DA_V_HEAD_DIM:DA_V_HEAD_DIM + 1]
        o2 = acc[tq:2 * tq, 0:DA_V_HEAD_DIM] / acc[tq:2 * tq, DA_V_HEAD_DIM:DA_V_HEAD_DIM + 1]
        lam_p = lam_ref[...]
        lam = (jnp.exp(jnp.sum(lam_p[0:1, :] * lam_p[1:2, :], axis=1, keepdims=True))
               - jnp.exp(jnp.sum(lam_p[2:3, :] * lam_p[3:4, :], axis=1, keepdims=True)) + lam_init)
        o = o1 - lam * o2
        r = lax.rsqrt(jnp.mean(o * o, axis=-1, keepdims=True) + EPS)
        o_ref[...] = ((o * r) * subw_ref[...] * (1.0 - lam_init)).astype(o_ref.dtype)


def _attn(P, lam_p, subw, batch, seq, tq, lam_init):
    T = P.shape[0]
    nq = seq // tq
    q_col = J_DA_Q * (SEG // 128)
    k_col = J_DA_K * (SEG // 128)
    v_col = J_DA_V * (SEG // 128)
    return pl.pallas_call(
        functools.partial(_attn_kernel, lam_init=lam_init),
        grid=(batch, DA_HEADS, nq, nq),
        in_specs=[
            pl.BlockSpec((tq, 128), lambda b, h, qi, ki: (b * nq + qi, q_col + h)),
            pl.BlockSpec((tq, 128), lambda b, h, qi, ki: (b * nq + jnp.minimum(ki, qi), k_col + h)),
            pl.BlockSpec((tq, 128), lambda b, h, qi, ki: (b * nq + jnp.minimum(ki, qi), v_col + h)),
            pl.BlockSpec((4, DA_HEAD_DIM), lambda b, h, qi, ki: (0, 0)),
            pl.BlockSpec((1, DA_V_HEAD_DIM), lambda b, h, qi, ki: (0, 0)),
        ],
        out_specs=pl.BlockSpec((tq, 128), lambda b, h, qi, ki: (b * nq + qi, h)),
        out_shape=jax.ShapeDtypeStruct((T, DA_HEADS * DA_V_HEAD_DIM), BF16),
        scratch_shapes=[
            pltpu.VMEM((2 * tq, 128), BF16),
            pltpu.VMEM((2 * tq, 1), F32),
            pltpu.VMEM((2 * tq, 2 * DA_V_HEAD_DIM), F32),
        ],
        compiler_params=_cparams(("arbitrary", "arbitrary", "arbitrary", "arbitrary")),
        name="attn",
    )(P, P, P, lam_p, subw)


def _ret_kernel(q_ref, k_ref, v_ref, g_ref, dec_ref, xi_ref, zeta_ref, cd_ref, rw_ref, o_ref, st_scr):
    n = pl.program_id(2)
    ts = q_ref.shape[0]
    C = RET_CHUNK

    @pl.when(n == 0)
    def _():
        st_scr[...] = jnp.zeros(st_scr.shape, F32)

    decay = dec_ref[0]
    xi = xi_ref[0]
    zeta = zeta_ref[0]
    cd = cd_ref[0]
    for c in range(ts // C):
        rows = slice(c * C, (c + 1) * C)
        qc = q_ref[rows, :]
        kc = k_ref[rows, :]
        vc = v_ref[rows, :]
        inner = lax.dot_general(qc, kc, (((1,), (1,)), ((), ())), preferred_element_type=F32) * decay
        st = st_scr[...]
        qx = (qc.astype(F32) * xi).astype(BF16)
        o = (jnp.dot(inner.astype(BF16), vc, preferred_element_type=F32)
             + jnp.dot(qx, st.astype(BF16), preferred_element_type=F32))
        kz = (kc.astype(F32) * zeta).astype(BF16)
        st_scr[...] = cd * st + lax.dot_general(kz, vc, (((0,), (0,)), ((), ())), preferred_element_type=F32)
        r = lax.rsqrt(jnp.mean(o * o, axis=-1, keepdims=True) + EPS)
        o_ref[rows, :] = (g_ref[rows, :].astype(F32) * ((o * r) * rw_ref[...])).astype(o_ref.dtype)


def _ret(P, decay, xi, zeta, cd, rw, batch, seq, ts):
    T = P.shape[0]
    ns = seq // ts
    q_col = J_RET_Q * (SEG // RET_KEY_DIM)
    k_col = J_RET_K * (SEG // RET_KEY_DIM)
    v_col = 5 * (SEG // RET_VAL_DIM)
    g_col = J_RET_G0 * (SEG // RET_VAL_DIM)
    C = RET_CHUNK
    return pl.pallas_call(
        _ret_kernel,
        grid=(batch, RET_HEADS, ns),
        in_specs=[
            pl.BlockSpec((ts, RET_KEY_DIM), lambda b, h, n: (b * ns + n, q_col + h)),
            pl.BlockSpec((ts, RET_KEY_DIM), lambda b, h, n: (b * ns + n, k_col + h)),
            pl.BlockSpec((ts, RET_VAL_DIM), lambda b, h, n: (b * ns + n, v_col + h)),
            pl.BlockSpec((ts, RET_VAL_DIM), lambda b, h, n: (b * ns + n, g_col + h)),
            pl.BlockSpec((1, C, C), lambda b, h, n: (h, 0, 0)),
            pl.BlockSpec((1, C, 1), lambda b, h, n: (h, 0, 0)),
            pl.BlockSpec((1, C, 1), lambda b, h, n: (h, 0, 0)),
            pl.BlockSpec((1, 1, 1), lambda b, h, n: (h, 0, 0)),
            pl.BlockSpec((1, RET_VAL_DIM), lambda b, h, n: (0, 0)),
        ],
        out_specs=pl.BlockSpec((ts, RET_VAL_DIM), lambda b, h, n: (b * ns + n, h)),
        out_shape=jax.ShapeDtypeStruct((T, RET_HEADS * RET_VAL_DIM), BF16),
        scratch_shapes=[pltpu.VMEM((RET_KEY_DIM, RET_VAL_DIM), F32)],
        compiler_params=_cparams(("arbitrary", "arbitrary", "arbitrary")),
        name="ret",
    )(P, P, P, P, decay, xi, zeta, cd, rw)


def _merge_kernel(x_ref, oa_ref, rb_ref, ga_ref, gb_ref, wa_ref, wb_ref, wo_ref, mnw_ref,
                  wrh_ref, wrl_ref, br_ref, ltri_ref,
                  x1_ref, hm_ref, ri_ref, rw_ref, cnt_ref, carry_scr):
    i = pl.program_id(0)
    tm = x_ref.shape[0]

    @pl.when(i == 0)
    def _():
        carry_scr[...] = jnp.zeros(carry_scr.shape, F32)

    ya = jnp.dot(oa_ref[...], wa_ref[...], preferred_element_type=F32)
    yb = jnp.dot(rb_ref[...], wb_ref[...], preferred_element_type=F32)
    z = ga_ref[...].astype(F32) * ya + gb_ref[...].astype(F32) * yb
    x1 = x_ref[...] + jnp.dot(z.astype(BF16), wo_ref[...], preferred_element_type=F32)
    x1_ref[...] = x1
    r = lax.rsqrt(jnp.mean(x1 * x1, axis=-1, keepdims=True) + EPS)
    hm = (x1 * r) * mnw_ref[...]
    hm_ref[...] = hm

    hm_hi = hm.astype(BF16)
    hm_lo = (hm - hm_hi.astype(F32)).astype(BF16)
    logits = (jnp.dot(hm_hi, wrh_ref[...], preferred_element_type=F32)
              + jnp.dot(hm_lo, wrh_ref[...], preferred_element_type=F32)
              + jnp.dot(hm_hi, wrl_ref[...], preferred_element_type=F32)) + br_ref[...]

    lane = lax.broadcasted_iota(I32, (tm, ROUTE_LANES), 1)
    lane_f = lane.astype(F32)

    def first_argmax(vals, vmax):
        return jnp.min(jnp.where(vals == vmax, lane_f, float(ROUTE_LANES)), axis=1, keepdims=True).astype(I32)

    gmask = lane < N_GROUPS
    lg = jnp.where(gmask, logits, NEG_BIG)
    mg = jnp.max(lg, axis=1, keepdims=True)
    g_idx = first_argmax(lg, mg)
    p_g = 1.0 / jnp.sum(jnp.where(gmask, jnp.exp(lg - mg), 0.0), axis=1, keepdims=True)

    lo_lane = N_GROUPS + EXPERTS_PER_GROUP * g_idx
    emask = (lane >= lo_lane) & (lane < lo_lane + EXPERTS_PER_GROUP)
    le = jnp.where(emask, logits, NEG_BIG)
    m1 = jnp.max(le, axis=1, keepdims=True)
    i1 = first_argmax(le, m1)
    le2 = jnp.where(lane == i1, NEG_BIG, le)
    m2 = jnp.max(le2, axis=1, keepdims=True)
    i2 = first_argmax(le2, m2)
    e21 = jnp.exp(m2 - m1)
    w_top1 = p_g / (1.0 + e21)
    w_top2 = p_g * e21 / (1.0 + e21)

    a1 = i1 - lo_lane
    a2 = i2 - lo_lane
    first_is_lo = a1 < a2
    a_lo = jnp.where(first_is_lo, a1, a2)
    a_hi = jnp.where(first_is_lo, a2, a1)
    w_lo = jnp.where(first_is_lo, w_top1, w_top2)
    w_hi = jnp.where(first_is_lo, w_top2, w_top1)
    pair = ((a_lo * (2 * EXPERTS_PER_GROUP - 1 - a_lo)) >> 1) + (a_hi - a_lo - 1)
    bucket = g_idx * PAIRS_PER_GROUP + pair

    oh = jnp.where(lane == bucket, 1.0, 0.0)
    prefix = jnp.dot(ltri_ref[...], oh.astype(BF16), preferred_element_type=F32) + carry_scr[...]
    rank = jnp.sum(jnp.where(lane == bucket, prefix, 0.0), axis=1, keepdims=True).astype(I32)
    carry = carry_scr[...] + jnp.sum(oh, axis=0, keepdims=True)
    carry_scr[...] = carry
    cnt_ref[...] = carry

    zero_i = jnp.zeros((tm, ROUTE_LANES), I32)
    ri_ref[...] = jnp.where(lane == 0, bucket, jnp.where(lane == 1, rank, zero_i))
    rw_ref[...] = jnp.where(lane == 0, w_lo, jnp.where(lane == 1, w_hi, 0.0))


def _merge(x2, oa, rb, P, wa, wb, wo, mnw, wr_hi, wr_lo, br, ltri, tm):
    T = x2.shape[0]
    full = lambda shape: pl.BlockSpec(shape, lambda i: (0,) * len(shape))
    return pl.pallas_call(
        _merge_kernel,
        grid=(T // tm,),
        in_specs=[
            pl.BlockSpec((tm, D_MODEL), lambda i: (i, 0)),
            pl.BlockSpec((tm, D_MODEL), lambda i: (i, 0)),
            pl.BlockSpec((tm, 2 * D_MODEL), lambda i: (i, 0)),
            pl.BlockSpec((tm, SEG), lambda i: (i, J_GATE_A)),
            pl.BlockSpec((tm, SEG), lambda i: (i, J_GATE_B)),
            full((D_MODEL, D_MODEL)),
            full((2 * D_MODEL, D_MODEL)),
            full((D_MODEL, D_MODEL)),
            full((1, D_MODEL)),
            full((D_MODEL, ROUTE_LANES)),
            full((D_MODEL, ROUTE_LANES)),
            full((1, ROUTE_LANES)),
            full((tm, tm)),
        ],
        out_specs=[
            pl.BlockSpec((tm, D_MODEL), lambda i: (i, 0)),
            pl.BlockSpec((tm, D_MODEL), lambda i: (i, 0)),
            pl.BlockSpec((tm, ROUTE_LANES), lambda i: (i, 0)),
            pl.BlockSpec((tm, ROUTE_LANES), lambda i: (i, 0)),
            full((1, ROUTE_LANES)),
        ],
        out_shape=[
            jax.ShapeDtypeStruct((T, D_MODEL), F32),
            jax.ShapeDtypeStruct((T, D_MODEL), F32),
            jax.ShapeDtypeStruct((T, ROUTE_LANES), I32),
            jax.ShapeDtypeStruct((T, ROUTE_LANES), F32),
            jax.ShapeDtypeStruct((1, ROUTE_LANES), F32),
        ],
        scratch_shapes=[pltpu.VMEM((1, ROUTE_LANES), F32)],
        compiler_params=_cparams(("arbitrary",)),
        name="merge",
    )(x2, oa, rb, P, P, wa, wb, wo, mnw, wr_hi, wr_lo, br, ltri)


def _dispatch_kernel(dest_ref, hm_ref, xs_in_ref, xs_ref, sem):
    del xs_in_ref
    i = pl.program_id(0)
    tm = hm_ref.shape[0]
    base = i * tm

    def row_copy(r):
        d = dest_ref[base + r]
        return pltpu.make_async_copy(hm_ref.at[pl.ds(r, 1), :], xs_ref.at[pl.ds(d, 1), :], sem)

    for r in range(tm):
        row_copy(r).start()
    for r in range(tm):
        row_copy(r).wait()


def _dispatch(dest, hm, xs_zero, tm):
    T = hm.shape[0]
    return pl.pallas_call(
        _dispatch_kernel,
        grid_spec=pltpu.PrefetchScalarGridSpec(
            num_scalar_prefetch=1,
            grid=(T // tm,),
            in_specs=[
                pl.BlockSpec((tm, D_MODEL), lambda i, dest: (i, 0)),
                pl.BlockSpec(memory_space=pl.ANY),
            ],
            out_specs=pl.BlockSpec(memory_space=pl.ANY),
            scratch_shapes=[pltpu.SemaphoreType.DMA(())],
        ),
        out_shape=jax.ShapeDtypeStruct(xs_zero.shape, xs_zero.dtype),
        input_output_aliases={2: 0},
        compiler_params=_cparams(("arbitrary",)),
        name="dispatch",
    )(dest, hm, xs_zero)


def _experts_kernel(tlo_ref, thi_ref, tvalid_ref, x_ref, w1l_ref, w3l_ref, w2l_ref,
                    w1h_ref, w3h_ref, w2h_ref, o_ref):
    del tlo_ref, thi_ref
    i = pl.program_id(0)

    @pl.when(tvalid_ref[i] == 1)
    def _():
        x = x_ref[...].astype(BF16)

        def mlp(w1_ref, w3_ref, w2_ref):
            a = jnp.dot(x, w1_ref[0], preferred_element_type=F32)
            b = jnp.dot(x, w3_ref[0], preferred_element_type=F32)
            h = (a * jax.nn.sigmoid(a)) * b
            return jnp.dot(h.astype(BF16), w2_ref[0], preferred_element_type=F32)

        o_ref[:, 0:D_MODEL] = mlp(w1l_ref, w3l_ref, w2l_ref)
        o_ref[:, D_MODEL:2 * D_MODEL] = mlp(w1h_ref, w3h_ref, w2h_ref)

    @pl.when(tvalid_ref[i] == 0)
    def _():
        o_ref[...] = jnp.zeros(o_ref.shape, o_ref.dtype)


def _experts(tile_lo, tile_hi, tile_valid, xs, w1, w3, w2, tr):
    R = xs.shape[0]
    w_up = lambda which: pl.BlockSpec(
        (1, D_MODEL, D_FF), lambda i, tlo, thi, tv: ((tlo if which == 0 else thi)[i], 0, 0))
    w_dn = lambda which: pl.BlockSpec(
        (1, D_FF, D_MODEL), lambda i, tlo, thi, tv: ((tlo if which == 0 else thi)[i], 0, 0))
    return pl.pallas_call(
        _experts_kernel,
        grid_spec=pltpu.PrefetchScalarGridSpec(
            num_scalar_prefetch=3,
            grid=(R // tr,),
            in_specs=[
                pl.BlockSpec((tr, D_MODEL), lambda i, tlo, thi, tv: (i, 0)),
                w_up(0), w_up(0), w_dn(0), w_up(1), w_up(1), w_dn(1),
            ],
            out_specs=pl.BlockSpec((tr, 2 * D_MODEL), lambda i, tlo, thi, tv: (i, 0)),
        ),
        out_shape=jax.ShapeDtypeStruct((R, 2 * D_MODEL), F32),
        compiler_params=_cparams(("arbitrary",)),
        name="experts",
    )(tile_lo, tile_hi, tile_valid, xs, w1, w3, w2, w1, w3, w2)


def _combine_kernel(dest_ref, x1_ref, rw_ref, ys_ref, o_ref, buf, sem):
    i = pl.program_id(0)
    tm = x1_ref.shape[0]
    base = i * tm

    def row_copy(r):
        d = dest_ref[base + r]
        return pltpu.make_async_copy(ys_ref.at[pl.ds(d, 1), :], buf.at[pl.ds(r, 1), :], sem)

    for r in range(tm):
        row_copy(r).start()
    for r in range(tm):
        row_copy(r).wait()

    rw = rw_ref[...]
    o_ref[...] = (x1_ref[...] + rw[:, 0:1] * buf[:, 0:D_MODEL]
                  + rw[:, 1:2] * buf[:, D_MODEL:2 * D_MODEL])


def _combine(dest, x1, rw, ys, tm):
    T = x1.shape[0]
    return pl.pallas_call(
        _combine_kernel,
        grid_spec=pltpu.PrefetchScalarGridSpec(
            num_scalar_prefetch=1,
            grid=(T // tm,),
            in_specs=[
                pl.BlockSpec((tm, D_MODEL), lambda i, dest: (i, 0)),
                pl.BlockSpec((tm, ROUTE_LANES), lambda i, dest: (i, 0)),
                pl.BlockSpec(memory_space=pl.ANY),
            ],
            out_specs=pl.BlockSpec((tm, D_MODEL), lambda i, dest: (i, 0)),
            scratch_shapes=[pltpu.VMEM((tm, 2 * D_MODEL), F32), pltpu.SemaphoreType.DMA(())],
        ),
        out_shape=jax.ShapeDtypeStruct((T, D_MODEL), F32),
        compiler_params=_cparams(("arbitrary",)),
        name="combine",
    )(dest, x1, rw, ys)


def _rotation_tables(seq):
    angle = 1.0 / (RET_ROT_BASE ** jnp.linspace(0.0, 1.0, RET_KEY_DIM // 2, dtype=F32))
    angle = jnp.repeat(angle, 2)
    ang = jnp.arange(seq).astype(F32)[:, None] * angle[None, :]
    return jnp.cos(ang), jnp.sin(ang)


def _retention_tables():
    C = RET_CHUNK
    log_g = jnp.log1p(-jnp.exp2(-5.0 - jnp.arange(RET_HEADS, dtype=F32)))
    j = jnp.arange(C, dtype=F32)
    rel = j[:, None] - j[None, :]
    decay = jnp.where(rel >= 0, jnp.exp(log_g[:, None, None] * jnp.maximum(rel, 0.0)), 0.0)
    xi = jnp.exp(log_g[:, None] * (j + 1.0))[:, :, None]
    zeta = jnp.exp(log_g[:, None] * (C - 1.0 - j))[:, :, None]
    cd = jnp.exp(log_g * C)[:, None, None]
    return decay, xi, zeta, cd


def _pair_tables():
    lo, hi = [], []
    for g in range(N_GROUPS):
        for a in range(EXPERTS_PER_GROUP):
            for b in range(a + 1, EXPERTS_PER_GROUP):
                lo.append(g * EXPERTS_PER_GROUP + a)
                hi.append(g * EXPERTS_PER_GROUP + b)
    return np.asarray(lo, np.int32), np.asarray(hi, np.int32)


def _pick(pref, n):
    t = min(pref, n)
    while n % t:
        t //= 2
    return t


def kernel(x, attn_norm_w, w_in, q_norm_w, k_norm_w, lambda_q1, lambda_k1, lambda_q2, lambda_k2,
           da_subln_w, ret_norm_w, w_branch_a, w_branch_b, w_out, moe_norm_w,
           w_group_router, b_group_router, w_expert_router, b_expert_router, w1, w3, w2):
    B, S, D = x.shape
    T = B * S
    depth = attn_norm_w.shape[0]

    tm_proj = _pick(1024, S)
    tq = _pick(512, S)
    ts_ret = _pick(512, S)
    tm_merge = _pick(512, T)
    tm_row = _pick(256, T)
    tr = 256

    cos_t, sin_t = _rotation_tables(S)
    decay, xi, zeta, cd = _retention_tables()
    gidx = np.arange(256) // DA_HEAD_DIM
    gsum = jnp.asarray((gidx[:, None] == gidx[None, :]).astype(np.float32) / DA_HEAD_DIM, BF16)
    ltri = jnp.asarray(np.tril(np.ones((tm_merge, tm_merge), np.float32), -1), BF16)
    pair_lo, pair_hi = _pair_tables()
    n_tiles = T // tr + N_BUCKETS
    R = n_tiles * tr

    x2 = x.reshape(T, D)
    for l in range(depth):
        lam_init = 0.8 - 0.6 * math.exp(-0.3 * l)
        P = _proj(x2, attn_norm_w[l][None, :], w_in[l].astype(BF16), cos_t, sin_t,
                  jnp.tile(q_norm_w[l], SEG // DA_HEAD_DIM)[None, :],
                  jnp.tile(k_norm_w[l], SEG // DA_HEAD_DIM)[None, :], gsum, S, tm_proj)
        lam_p = jnp.stack([lambda_q1[l], lambda_k1[l], lambda_q2[l], lambda_k2[l]]).astype(F32)
        oa = _attn(P, lam_p, da_subln_w[l][None, :], B, S, tq, lam_init)
        rb = _ret(P, decay, xi, zeta, cd, ret_norm_w[l][None, :].astype(F32), B, S, ts_ret)

        w_router = jnp.zeros((D, ROUTE_LANES), F32)
        w_router = w_router.at[:, 0:N_GROUPS].set(w_group_router[l])
        w_router = w_router.at[:, N_GROUPS:N_GROUPS + N_EXPERTS].set(w_expert_router[l])
        wr_hi = w_router.astype(BF16)
        wr_lo = (w_router - wr_hi.astype(F32)).astype(BF16)
        b_router = jnp.zeros((1, ROUTE_LANES), F32)
        b_router = b_router.at[0, 0:N_GROUPS].set(b_group_router[l])
        b_router = b_router.at[0, N_GROUPS:N_GROUPS + N_EXPERTS].set(b_expert_router[l].reshape(-1))

        x1, hm, ri, rw, cnt = _merge(
            x2, oa, rb, P, w_branch_a[l].astype(BF16), w_branch_b[l].astype(BF16),
            w_out[l].astype(BF16), moe_norm_w[l][None, :], wr_hi, wr_lo, b_router, ltri, tm_merge)

        counts = cnt[0, 0:N_BUCKETS].astype(I32)
        tiles_per = (counts + tr - 1) // tr
        tile_end = jnp.cumsum(tiles_per)
        seg_start = (tile_end - tiles_per) * tr
        dest = seg_start[ri[:, 0]] + ri[:, 1]
        tile_ids = jnp.arange(n_tiles, dtype=I32)
        n_used = tile_end[-1]
        tile_bucket = jnp.searchsorted(tile_end, jnp.minimum(tile_ids, n_used - 1), side="right").astype(I32)
        tile_bucket = jnp.minimum(tile_bucket, N_BUCKETS - 1)
        tile_lo = jnp.asarray(pair_lo)[tile_bucket]
        tile_hi = jnp.asarray(pair_hi)[tile_bucket]
        tile_valid = (tile_ids < n_used).astype(I32)

        xs = _dispatch(dest, hm, jnp.zeros((R, D), F32), tm_row)
        ys = _experts(tile_lo, tile_hi, tile_valid, xs, w1[l].astype(BF16), w3[l].astype(BF16),
                      w2[l].astype(BF16), tr)
        x2 = _combine(dest, x1, rw, ys, tm_row)
    return x2.reshape(B, S, D)
```

```python
import functools
import math

import numpy as np
import jax
import jax.numpy as jnp
from jax import lax
from jax.experimental import pallas as pl
from jax.experimental.pallas import tpu as pltpu

F32 = jnp.float32
BF16 = jnp.bfloat16
I32 = jnp.int32

D_MODEL = 1024
EPS = 1e-6

DA_HEADS = 8
DA_HEAD_DIM = 64
DA_V_HEAD_DIM = 128

RET_HEADS = 4
RET_KEY_DIM = 256
RET_VAL_DIM = 512
RET_CHUNK = 128
RET_ROT_BASE = 10000.0

SEG = 1024
IN_WIDTH = 11 * SEG
J_DA_Q, J_DA_K, J_DA_V, J_RET_Q, J_RET_K = 0, 1, 2, 3, 4
J_RET_G0, J_RET_G1, J_GATE_A, J_GATE_B = 7, 8, 9, 10

N_GROUPS = 4
EXPERTS_PER_GROUP = 8
N_EXPERTS = 32
D_FF = 512
PAIRS_PER_GROUP = 28
N_BUCKETS = N_GROUPS * PAIRS_PER_GROUP
ROUTE_LANES = 128
RANK_BITS = 17

NEG_BIG = -1e30
LOG2E = 1.4426950408889634

VMEM_LIMIT = 56 * 1024 * 1024


def _cparams(sem):
    return pltpu.CompilerParams(dimension_semantics=sem, vmem_limit_bytes=VMEM_LIMIT)


def _proj_kernel(x_ref, nw_ref, w_ref, cos_ref, sin_ref, qnw_ref, knw_ref, gsum_ref, o_ref, h_scr):
    j = pl.program_id(1)
    tm = x_ref.shape[0]

    @pl.when(j == 0)
    def _():
        x = x_ref[...]
        r = lax.rsqrt(jnp.mean(x * x, axis=-1, keepdims=True) + EPS)
        h_scr[...] = ((x * r) * nw_ref[...]).astype(BF16)

    y = jnp.dot(h_scr[...], w_ref[...], preferred_element_type=F32)

    def qk_norm(w_ref_, scale):
        for c in range(SEG // 256):
            sl = slice(c * 256, (c + 1) * 256)
            yc = y[:, sl]
            ms = jnp.dot((yc * yc).astype(BF16), gsum_ref[...], preferred_element_type=F32)
            o_ref[:, sl] = ((yc * lax.rsqrt(ms + EPS)) * w_ref_[:, sl] * scale).astype(o_ref.dtype)

    @pl.when(j == J_DA_Q)
    def _():
        qk_norm(qnw_ref, DA_HEAD_DIM ** -0.5 * LOG2E)

    @pl.when(j == J_DA_K)
    def _():
        qk_norm(knw_ref, 1.0)

    def rotate(scale):
        lane = lax.broadcasted_iota(I32, (tm, RET_KEY_DIM), 1)
        even = (lane & 1) == 0
        cos = cos_ref[...]
        sin = sin_ref[...]
        for c in range(SEG // RET_KEY_DIM):
            sl = slice(c * RET_KEY_DIM, (c + 1) * RET_KEY_DIM)
            t = y[:, sl]
            t_next = pltpu.roll(t, RET_KEY_DIM - 1, 1)
            t_prev = pltpu.roll(t, 1, 1)
            rot = jnp.where(even, -t_next, t_prev)
            o_ref[:, sl] = ((t * cos + rot * sin) * scale).astype(o_ref.dtype)

    @pl.when(j == J_RET_Q)
    def _():
        rotate(1.0)

    @pl.when(j == J_RET_K)
    def _():
        rotate(RET_KEY_DIM ** -0.5)

    @pl.when((j == J_DA_V) | (j == 5) | (j == 6))
    def _():
        o_ref[...] = y.astype(o_ref.dtype)

    @pl.when((j == J_RET_G0) | (j == J_RET_G1))
    def _():
        o_ref[...] = (y * jax.nn.sigmoid(y)).astype(o_ref.dtype)

    @pl.when((j == J_GATE_A) | (j == J_GATE_B))
    def _():
        o_ref[...] = jax.nn.sigmoid(y).astype(o_ref.dtype)


def _proj(x2, attn_norm_w, w_in_bf, cos_t, sin_t, qnw_t, knw_t, gsum, seq, tm):
    T = x2.shape[0]
    n_pos_blocks = seq // tm
    return pl.pallas_call(
        _proj_kernel,
        grid=(T // tm, IN_WIDTH // SEG),
        in_specs=[
            pl.BlockSpec((tm, D_MODEL), lambda i, j: (i, 0)),
            pl.BlockSpec((1, D_MODEL), lambda i, j: (0, 0)),
            pl.BlockSpec((D_MODEL, SEG), lambda i, j: (0, j)),
            pl.BlockSpec((tm, RET_KEY_DIM), lambda i, j: (i % n_pos_blocks, 0)),
            pl.BlockSpec((tm, RET_KEY_DIM), lambda i, j: (i % n_pos_blocks, 0)),
            pl.BlockSpec((1, SEG), lambda i, j: (0, 0)),
            pl.BlockSpec((1, SEG), lambda i, j: (0, 0)),
            pl.BlockSpec((256, 256), lambda i, j: (0, 0)),
        ],
        out_specs=pl.BlockSpec((tm, SEG), lambda i, j: (i, j)),
        out_shape=jax.ShapeDtypeStruct((T, IN_WIDTH), BF16),
        scratch_shapes=[pltpu.VMEM((tm, D_MODEL), BF16)],
        compiler_params=_cparams(("arbitrary", "arbitrary")),
        name="proj",
    )(x2, attn_norm_w, w_in_bf, cos_t, sin_t, qnw_t, knw_t, gsum)


def _attn_kernel(q_ref, k_ref, v_ref, lam_ref, subw_ref, o_ref, q2_scr, s_scr, m_scr, acc_scr, *,
                 lam_init, strip):
    qi = pl.program_id(2)
    tq = q_ref.shape[0]
    tk = tq
    dv = DA_V_HEAD_DIM

    q = q_ref[...]
    lane = lax.broadcasted_iota(I32, q.shape, 1)
    zero = jnp.zeros_like(q)
    q2_scr[0:tq, :] = jnp.where(lane < DA_HEAD_DIM, q, zero)
    q2_scr[tq:2 * tq, :] = jnp.where(lane >= DA_HEAD_DIM, q, zero)
    m_scr[...] = jnp.full(m_scr.shape, NEG_BIG, F32)
    acc_scr[...] = jnp.zeros(acc_scr.shape, F32)
    ones = jnp.ones((tk, dv), BF16)

    def scores(j, slot):
        off = pl.multiple_of(j * tk, tk)
        s_scr[slot] = lax.dot_general(q2_scr[...], k_ref[pl.ds(off, tk), :], (((1,), (1,)), ((), ())),
                                      preferred_element_type=F32)

    def softmax_pv(j, slot, masked):
        off = pl.multiple_of(j * tk, tk)
        vext = jnp.concatenate([v_ref[pl.ds(off, tk), :], ones], axis=1)
        for r in range(2 * tq // strip):
            rows = slice(r * strip, (r + 1) * strip)
            s = s_scr[slot, rows, :]
            if masked:
                qrow = (r * strip) % tq + lax.broadcasted_iota(I32, (strip, tk), 0)
                col = lax.broadcasted_iota(I32, (strip, tk), 1)
                s = jnp.where(col > qrow, NEG_BIG, s)
            m_prev = m_scr[rows, :]
            m_new = jnp.maximum(m_prev, jnp.max(s, axis=1, keepdims=True))
            alpha = jnp.exp2(m_prev - m_new)
            p = jnp.exp2(s - jnp.concatenate([m_new] * (tk // 128), axis=1))
            pv = jnp.dot(p.astype(BF16), vext, preferred_element_type=F32)
            acc_scr[rows, :] = jnp.concatenate([alpha, alpha], axis=1) * acc_scr[rows, :] + pv
            m_scr[rows, :] = m_new

    scores(0, 0)
    n_pairs = qi // 2

    def body(i, carry):
        j = 2 * i
        scores(j + 1, 1)
        softmax_pv(j, 0, False)
        scores(j + 2, 0)
        softmax_pv(j + 1, 1, False)
        return carry

    lax.fori_loop(0, n_pairs, body, 0)

    @pl.when(qi % 2 == 1)
    def _():
        scores(qi, 1)
        softmax_pv(qi - 1, 0, False)
        softmax_pv(qi, 1, True)

    @pl.when(qi % 2 == 0)
    def _():
        softmax_pv(qi, 0, True)

    o1 = acc_scr[0:tq, 0:dv] / acc_scr[0:tq, dv:2 * dv]
    o2 = acc_scr[tq:2 * tq, 0:dv] / acc_scr[tq:2 * tq, dv:2 * dv]
    lam_p = lam_ref[...]
    lam = (jnp.exp(jnp.sum(lam_p[0:1, :] * lam_p[1:2, :], axis=1, keepdims=True))
           - jnp.exp(jnp.sum(lam_p[2:3, :] * lam_p[3:4, :], axis=1, keepdims=True)) + lam_init)
    o = o1 - lam * o2
    r = lax.rsqrt(jnp.mean(o * o, axis=-1, keepdims=True) + EPS)
    o_ref[...] = ((o * r) * subw_ref[...] * (1.0 - lam_init)).astype(o_ref.dtype)


def _attn(P, lam_p, subw, batch, seq, tq, lam_init):
    T = P.shape[0]
    nq = seq // tq
    q_col = J_DA_Q * (SEG // 128)
    k_col = J_DA_K * (SEG // 128)
    v_col = J_DA_V * (SEG // 128)
    return pl.pallas_call(
        functools.partial(_attn_kernel, lam_init=lam_init, strip=min(128, tq)),
        grid=(batch, DA_HEADS, nq),
        in_specs=[
            pl.BlockSpec((tq, 128), lambda b, h, qi: (b * nq + qi, q_col + h)),
            pl.BlockSpec((seq, 128), lambda b, h, qi: (b, k_col + h)),
            pl.BlockSpec((seq, 128), lambda b, h, qi: (b, v_col + h)),
            pl.BlockSpec((4, DA_HEAD_DIM), lambda b, h, qi: (0, 0)),
            pl.BlockSpec((1, DA_V_HEAD_DIM), lambda b, h, qi: (0, 0)),
        ],
        out_specs=pl.BlockSpec((tq, 128), lambda b, h, qi: (b * nq + qi, h)),
        out_shape=jax.ShapeDtypeStruct((T, DA_HEADS * DA_V_HEAD_DIM), BF16),
        scratch_shapes=[
            pltpu.VMEM((2 * tq, 128), BF16),
            pltpu.VMEM((2, 2 * tq, tq), F32),
            pltpu.VMEM((2 * tq, 128), F32),
            pltpu.VMEM((2 * tq, 2 * DA_V_HEAD_DIM), F32),
        ],
        compiler_params=_cparams(("arbitrary", "arbitrary", "arbitrary")),
        name="attn",
    )(P, P, P, lam_p, subw)


def _ret_kernel(q_ref, k_ref, v_ref, g_ref, dec_ref, xi_ref, zeta_ref, cd_ref, rw_ref, o_ref, st_scr):
    n = pl.program_id(2)
    ts = q_ref.shape[0]
    C = RET_CHUNK

    @pl.when(n == 0)
    def _():
        st_scr[...] = jnp.zeros(st_scr.shape, F32)

    decay = dec_ref[0]
    xi = xi_ref[0]
    zeta = zeta_ref[0]
    cd = cd_ref[0]
    for c in range(ts // C):
        rows = slice(c * C, (c + 1) * C)
        qc = q_ref[rows, :]
        kc = k_ref[rows, :]
        vc = v_ref[rows, :]
        inner = lax.dot_general(qc, kc, (((1,), (1,)), ((), ())), preferred_element_type=F32) * decay
        st = st_scr[...]
        qx = (qc.astype(F32) * xi).astype(BF16)
        o = (jnp.dot(inner.astype(BF16), vc, preferred_element_type=F32)
             + jnp.dot(qx, st.astype(BF16), preferred_element_type=F32))
        kz = (kc.astype(F32) * zeta).astype(BF16)
        st_scr[...] = cd * st + lax.dot_general(kz, vc, (((0,), (0,)), ((), ())), preferred_element_type=F32)
        r = lax.rsqrt(jnp.mean(o * o, axis=-1, keepdims=True) + EPS)
        o_ref[rows, :] = (g_ref[rows, :].astype(F32) * ((o * r) * rw_ref[...])).astype(o_ref.dtype)


def _ret(P, decay, xi, zeta, cd, rw, batch, seq, ts):
    T = P.shape[0]
    ns = seq // ts
    q_col = J_RET_Q * (SEG // RET_KEY_DIM)
    k_col = J_RET_K * (SEG // RET_KEY_DIM)
    v_col = 5 * (SEG // RET_VAL_DIM)
    g_col = J_RET_G0 * (SEG // RET_VAL_DIM)
    C = RET_CHUNK
    return pl.pallas_call(
        _ret_kernel,
        grid=(batch, RET_HEADS, ns),
        in_specs=[
            pl.BlockSpec((ts, RET_KEY_DIM), lambda b, h, n: (b * ns + n, q_col + h)),
            pl.BlockSpec((ts, RET_KEY_DIM), lambda b, h, n: (b * ns + n, k_col + h)),
            pl.BlockSpec((ts, RET_VAL_DIM), lambda b, h, n: (b * ns + n, v_col + h)),
            pl.BlockSpec((ts, RET_VAL_DIM), lambda b, h, n: (b * ns + n, g_col + h)),
            pl.BlockSpec((1, C, C), lambda b, h, n: (h, 0, 0)),
            pl.BlockSpec((1, C, 1), lambda b, h, n: (h, 0, 0)),
            pl.BlockSpec((1, C, 1), lambda b, h, n: (h, 0, 0)),
            pl.BlockSpec((1, 1, 1), lambda b, h, n: (h, 0, 0)),
            pl.BlockSpec((1, RET_VAL_DIM), lambda b, h, n: (0, 0)),
        ],
        out_specs=pl.BlockSpec((ts, RET_VAL_DIM), lambda b, h, n: (b * ns + n, h)),
        out_shape=jax.ShapeDtypeStruct((T, RET_HEADS * RET_VAL_DIM), BF16),
        scratch_shapes=[pltpu.VMEM((RET_KEY_DIM, RET_VAL_DIM), F32)],
        compiler_params=_cparams(("arbitrary", "arbitrary", "arbitrary")),
        name="ret",
    )(P, P, P, P, decay, xi, zeta, cd, rw)


def _merge_kernel(x_ref, oa_ref, rb_ref, ga_ref, gb_ref, wa_ref, wb_ref, wo_ref, mnw_ref,
                  wrh_ref, wrl_ref, br_ref, ltri_ref,
                  x1_ref, hm_ref, ri_ref, rw_ref, cnt_ref, carry_scr):
    i = pl.program_id(0)
    tm = x_ref.shape[0]

    @pl.when(i == 0)
    def _():
        carry_scr[...] = jnp.zeros(carry_scr.shape, F32)

    ya = jnp.dot(oa_ref[...], wa_ref[...], preferred_element_type=F32)
    yb = jnp.dot(rb_ref[...], wb_ref[...], preferred_element_type=F32)
    z = ga_ref[...].astype(F32) * ya + gb_ref[...].astype(F32) * yb
    x1 = x_ref[...] + jnp.dot(z.astype(BF16), wo_ref[...], preferred_element_type=F32)
    x1_ref[...] = x1
    r = lax.rsqrt(jnp.mean(x1 * x1, axis=-1, keepdims=True) + EPS)
    hm = (x1 * r) * mnw_ref[...]
    hm_ref[...] = hm

    hm_hi = hm.astype(BF16)
    hm_lo = (hm - hm_hi.astype(F32)).astype(BF16)
    logits = (jnp.dot(hm_hi, wrh_ref[...], preferred_element_type=F32)
              + jnp.dot(hm_lo, wrh_ref[...], preferred_element_type=F32)
              + jnp.dot(hm_hi, wrl_ref[...], preferred_element_type=F32)) + br_ref[...]

    lane = lax.broadcasted_iota(I32, (tm, ROUTE_LANES), 1)
    lane_f = lane.astype(F32)

    def first_argmax(vals, vmax):
        return jnp.min(jnp.where(vals == vmax, lane_f, float(ROUTE_LANES)), axis=1, keepdims=True).astype(I32)

    gmask = lane < N_GROUPS
    lg = jnp.where(gmask, logits, NEG_BIG)
    mg = jnp.max(lg, axis=1, keepdims=True)
    g_idx = first_argmax(lg, mg)
    p_g = 1.0 / jnp.sum(jnp.where(gmask, jnp.exp(lg - mg), 0.0), axis=1, keepdims=True)

    lo_lane = N_GROUPS + EXPERTS_PER_GROUP * g_idx
    emask = (lane >= lo_lane) & (lane < lo_lane + EXPERTS_PER_GROUP)
    le = jnp.where(emask, logits, NEG_BIG)
    m1 = jnp.max(le, axis=1, keepdims=True)
    i1 = first_argmax(le, m1)
    le2 = jnp.where(lane == i1, NEG_BIG, le)
    m2 = jnp.max(le2, axis=1, keepdims=True)
    i2 = first_argmax(le2, m2)
    e21 = jnp.exp(m2 - m1)
    w_top1 = p_g / (1.0 + e21)
    w_top2 = p_g * e21 / (1.0 + e21)

    a1 = i1 - lo_lane
    a2 = i2 - lo_lane
    first_is_lo = a1 < a2
    a_lo = jnp.where(first_is_lo, a1, a2)
    a_hi = jnp.where(first_is_lo, a2, a1)
    w_lo = jnp.where(first_is_lo, w_top1, w_top2)
    w_hi = jnp.where(first_is_lo, w_top2, w_top1)
    pair = ((a_lo * (2 * EXPERTS_PER_GROUP - 1 - a_lo)) >> 1) + (a_hi - a_lo - 1)
    bucket = g_idx * PAIRS_PER_GROUP + pair

    oh = jnp.where(lane == bucket, 1.0, 0.0)
    prefix = jnp.dot(ltri_ref[...], oh.astype(BF16), preferred_element_type=F32) + carry_scr[...]
    rank = jnp.sum(jnp.where(lane == bucket, prefix, 0.0), axis=1, keepdims=True).astype(I32)
    carry = carry_scr[...] + jnp.sum(oh, axis=0, keepdims=True)
    carry_scr[...] = carry
    cnt_ref[...] = carry

    zero_i = jnp.zeros((tm, ROUTE_LANES), I32)
    ri_ref[...] = jnp.where(lane == 0, (bucket << RANK_BITS) | rank, zero_i)
    rw_ref[...] = jnp.where(lane == 0, w_lo, jnp.where(lane == 1, w_hi, 0.0))


def _merge(x2, oa, rb, P, wa, wb, wo, mnw, wr_hi, wr_lo, br, ltri, tm):
    T = x2.shape[0]
    full = lambda shape: pl.BlockSpec(shape, lambda i: (0,) * len(shape))
    return pl.pallas_call(
        _merge_kernel,
        grid=(T // tm,),
        in_specs=[
            pl.BlockSpec((tm, D_MODEL), lambda i: (i, 0)),
            pl.BlockSpec((tm, D_MODEL), lambda i: (i, 0)),
            pl.BlockSpec((tm, 2 * D_MODEL), lambda i: (i, 0)),
            pl.BlockSpec((tm, SEG), lambda i: (i, J_GATE_A)),
            pl.BlockSpec((tm, SEG), lambda i: (i, J_GATE_B)),
            full((D_MODEL, D_MODEL)),
            full((2 * D_MODEL, D_MODEL)),
            full((D_MODEL, D_MODEL)),
            full((1, D_MODEL)),
            full((D_MODEL, ROUTE_LANES)),
            full((D_MODEL, ROUTE_LANES)),
            full((1, ROUTE_LANES)),
            full((tm, tm)),
        ],
        out_specs=[
            pl.BlockSpec((tm, D_MODEL), lambda i: (i, 0)),
            pl.BlockSpec((tm, D_MODEL), lambda i: (i, 0)),
            pl.BlockSpec((tm, ROUTE_LANES), lambda i: (i, 0)),
            pl.BlockSpec((tm, ROUTE_LANES), lambda i: (i, 0)),
            full((1, ROUTE_LANES)),
        ],
        out_shape=[
            jax.ShapeDtypeStruct((T, D_MODEL), F32),
            jax.ShapeDtypeStruct((T, D_MODEL), F32),
            jax.ShapeDtypeStruct((T, ROUTE_LANES), I32),
            jax.ShapeDtypeStruct((T, ROUTE_LANES), F32),
            jax.ShapeDtypeStruct((1, ROUTE_LANES), F32),
        ],
        scratch_shapes=[pltpu.VMEM((1, ROUTE_LANES), F32)],
        compiler_params=_cparams(("arbitrary",)),
        name="merge",
    )(x2, oa, rb, P, P, wa, wb, wo, mnw, wr_hi, wr_lo, br, ltri)


def _sorted_row(route_ref, seg_ref, t):
    word = route_ref[t]
    return seg_ref[word >> RANK_BITS] + (word & ((1 << RANK_BITS) - 1))


def _dispatch_kernel(route_ref, seg_ref, hm_ref, xs_in_ref, xs_ref, sem):
    del xs_in_ref
    i = pl.program_id(0)
    tm = hm_ref.shape[0]
    base = i * tm

    def row_copy(r):
        d = _sorted_row(route_ref, seg_ref, base + r)
        return pltpu.make_async_copy(hm_ref.at[pl.ds(r, 1), :], xs_ref.at[pl.ds(d, 1), :], sem)

    for r in range(tm):
        row_copy(r).start()
    for r in range(tm):
        row_copy(r).wait()


def _dispatch(route, seg_start, hm, xs_zero, tm):
    T = hm.shape[0]
    return pl.pallas_call(
        _dispatch_kernel,
        grid_spec=pltpu.PrefetchScalarGridSpec(
            num_scalar_prefetch=2,
            grid=(T // tm,),
            in_specs=[
                pl.BlockSpec((tm, D_MODEL), lambda i, route, seg: (i, 0)),
                pl.BlockSpec(memory_space=pl.ANY),
            ],
            out_specs=pl.BlockSpec(memory_space=pl.ANY),
            scratch_shapes=[pltpu.SemaphoreType.DMA(())],
        ),
        out_shape=jax.ShapeDtypeStruct(xs_zero.shape, xs_zero.dtype),
        input_output_aliases={3: 0},
        compiler_params=_cparams(("arbitrary",)),
        name="dispatch",
    )(route, seg_start, hm, xs_zero)


def _experts_kernel(tlo_ref, thi_ref, tvalid_ref, x_ref, w1l_ref, w3l_ref, w2l_ref,
                    w1h_ref, w3h_ref, w2h_ref, o_ref):
    del tlo_ref, thi_ref
    i = pl.program_id(0)

    @pl.when(tvalid_ref[i] == 1)
    def _():
        x = x_ref[...].astype(BF16)

        def mlp(w1_ref, w3_ref, w2_ref):
            a = jnp.dot(x, w1_ref[0], preferred_element_type=F32)
            b = jnp.dot(x, w3_ref[0], preferred_element_type=F32)
            h = (a * jax.nn.sigmoid(a)) * b
            return jnp.dot(h.astype(BF16), w2_ref[0], preferred_element_type=F32)

        o_ref[:, 0:D_MODEL] = mlp(w1l_ref, w3l_ref, w2l_ref)
        o_ref[:, D_MODEL:2 * D_MODEL] = mlp(w1h_ref, w3h_ref, w2h_ref)

    @pl.when(tvalid_ref[i] == 0)
    def _():
        o_ref[...] = jnp.zeros(o_ref.shape, o_ref.dtype)


def _experts(tile_lo, tile_hi, tile_valid, xs, w1, w3, w2, tr):
    R = xs.shape[0]
    w_up = lambda which: pl.BlockSpec(
        (1, D_MODEL, D_FF), lambda i, tlo, thi, tv: ((tlo if which == 0 else thi)[i], 0, 0))
    w_dn = lambda which: pl.BlockSpec(
        (1, D_FF, D_MODEL), lambda i, tlo, thi, tv: ((tlo if which == 0 else thi)[i], 0, 0))
    return pl.pallas_call(
        _experts_kernel,
        grid_spec=pltpu.PrefetchScalarGridSpec(
            num_scalar_prefetch=3,
            grid=(R // tr,),
            in_specs=[
                pl.BlockSpec((tr, D_MODEL), lambda i, tlo, thi, tv: (i, 0)),
                w_up(0), w_up(0), w_dn(0), w_up(1), w_up(1), w_dn(1),
            ],
            out_specs=pl.BlockSpec((tr, 2 * D_MODEL), lambda i, tlo, thi, tv: (i, 0)),
        ),
        out_shape=jax.ShapeDtypeStruct((R, 2 * D_MODEL), F32),
        compiler_params=_cparams(("arbitrary",)),
        name="experts",
    )(tile_lo, tile_hi, tile_valid, xs, w1, w3, w2, w1, w3, w2)


def _combine_kernel(route_ref, seg_ref, x1_ref, rw_ref, ys_ref, o_ref, buf, sem):
    i = pl.program_id(0)
    tm = x1_ref.shape[0]
    base = i * tm

    def row_copy(r):
        d = _sorted_row(route_ref, seg_ref, base + r)
        return pltpu.make_async_copy(ys_ref.at[pl.ds(d, 1), :], buf.at[pl.ds(r, 1), :], sem)

    for r in range(tm):
        row_copy(r).start()
    for r in range(tm):
        row_copy(r).wait()

    rw = rw_ref[...]
    o_ref[...] = (x1_ref[...] + rw[:, 0:1] * buf[:, 0:D_MODEL]
                  + rw[:, 1:2] * buf[:, D_MODEL:2 * D_MODEL])


def _combine(route, seg_start, x1, rw, ys, tm):
    T = x1.shape[0]
    return pl.pallas_call(
        _combine_kernel,
        grid_spec=pltpu.PrefetchScalarGridSpec(
            num_scalar_prefetch=2,
            grid=(T // tm,),
            in_specs=[
                pl.BlockSpec((tm, D_MODEL), lambda i, route, seg: (i, 0)),
                pl.BlockSpec((tm, ROUTE_LANES), lambda i, route, seg: (i, 0)),
                pl.BlockSpec(memory_space=pl.ANY),
            ],
            out_specs=pl.BlockSpec((tm, D_MODEL), lambda i, route, seg: (i, 0)),
            scratch_shapes=[pltpu.VMEM((tm, 2 * D_MODEL), F32), pltpu.SemaphoreType.DMA(())],
        ),
        out_shape=jax.ShapeDtypeStruct((T, D_MODEL), F32),
        compiler_params=_cparams(("arbitrary",)),
        name="combine",
    )(route, seg_start, x1, rw, ys)


def _rotation_tables(seq):
    angle = 1.0 / (RET_ROT_BASE ** jnp.linspace(0.0, 1.0, RET_KEY_DIM // 2, dtype=F32))
    angle = jnp.repeat(angle, 2)
    ang = jnp.arange(seq).astype(F32)[:, None] * angle[None, :]
    return jnp.cos(ang), jnp.sin(ang)


def _retention_tables():
    C = RET_CHUNK
    log_g = jnp.log1p(-jnp.exp2(-5.0 - jnp.arange(RET_HEADS, dtype=F32)))
    j = jnp.arange(C, dtype=F32)
    rel = j[:, None] - j[None, :]
    decay = jnp.where(rel >= 0, jnp.exp(log_g[:, None, None] * jnp.maximum(rel, 0.0)), 0.0)
    xi = jnp.exp(log_g[:, None] * (j + 1.0))[:, :, None]
    zeta = jnp.exp(log_g[:, None] * (C - 1.0 - j))[:, :, None]
    cd = jnp.exp(log_g * C)[:, None, None]
    return decay, xi, zeta, cd


def _pair_tables():
    lo, hi = [], []
    for g in range(N_GROUPS):
        for a in range(EXPERTS_PER_GROUP):
            for b in range(a + 1, EXPERTS_PER_GROUP):
                lo.append(g * EXPERTS_PER_GROUP + a)
                hi.append(g * EXPERTS_PER_GROUP + b)
    return np.asarray(lo, np.int32), np.asarray(hi, np.int32)


def _pick(pref, n):
    t = min(pref, n)
    while n % t:
        t //= 2
    return t


def kernel(x, attn_norm_w, w_in, q_norm_w, k_norm_w, lambda_q1, lambda_k1, lambda_q2, lambda_k2,
           da_subln_w, ret_norm_w, w_branch_a, w_branch_b, w_out, moe_norm_w,
           w_group_router, b_group_router, w_expert_router, b_expert_router, w1, w3, w2):
    B, S, D = x.shape
    T = B * S
    assert T <= (1 << RANK_BITS)
    depth = attn_norm_w.shape[0]

    tm_proj = _pick(1024, S)
    tq = _pick(512, S)
    ts_ret = _pick(512, S)
    tm_merge = _pick(512, T)
    tm_row = _pick(256, T)
    tr = 256

    cos_t, sin_t = _rotation_tables(S)
    decay, xi, zeta, cd = _retention_tables()
    gidx = np.arange(256) // DA_HEAD_DIM
    gsum = jnp.asarray((gidx[:, None] == gidx[None, :]).astype(np.float32) / DA_HEAD_DIM, BF16)
    ltri = jnp.asarray(np.tril(np.ones((tm_merge, tm_merge), np.float32), -1), BF16)
    pair_lo, pair_hi = _pair_tables()
    n_tiles = T // tr + N_BUCKETS
    R = n_tiles * tr

    x2 = x.reshape(T, D)
    for l in range(depth):
        lam_init = 0.8 - 0.6 * math.exp(-0.3 * l)
        P = _proj(x2, attn_norm_w[l][None, :], w_in[l].astype(BF16), cos_t, sin_t,
                  jnp.tile(q_norm_w[l], SEG // DA_HEAD_DIM)[None, :],
                  jnp.tile(k_norm_w[l], SEG // DA_HEAD_DIM)[None, :], gsum, S, tm_proj)
        lam_p = jnp.stack([lambda_q1[l], lambda_k1[l], lambda_q2[l], lambda_k2[l]]).astype(F32)
        oa = _attn(P, lam_p, da_subln_w[l][None, :], B, S, tq, lam_init)
        rb = _ret(P, decay, xi, zeta, cd, ret_norm_w[l][None, :].astype(F32), B, S, ts_ret)

        w_router = jnp.zeros((D, ROUTE_LANES), F32)
        w_router = w_router.at[:, 0:N_GROUPS].set(w_group_router[l])
        w_router = w_router.at[:, N_GROUPS:N_GROUPS + N_EXPERTS].set(w_expert_router[l])
        wr_hi = w_router.astype(BF16)
        wr_lo = (w_router - wr_hi.astype(F32)).astype(BF16)
        b_router = jnp.zeros((1, ROUTE_LANES), F32)
        b_router = b_router.at[0, 0:N_GROUPS].set(b_group_router[l])
        b_router = b_router.at[0, N_GROUPS:N_GROUPS + N_EXPERTS].set(b_expert_router[l].reshape(-1))

        x1, hm, ri, rw, cnt = _merge(
            x2, oa, rb, P, w_branch_a[l].astype(BF16), w_branch_b[l].astype(BF16),
            w_out[l].astype(BF16), moe_norm_w[l][None, :], wr_hi, wr_lo, b_router, ltri, tm_merge)

        counts = cnt[0, 0:N_BUCKETS].astype(I32)
        tiles_per = (counts + tr - 1) // tr
        tile_end = jnp.cumsum(tiles_per)
        seg_start = (tile_end - tiles_per) * tr
        route = ri[:, 0]
        tile_ids = jnp.arange(n_tiles, dtype=I32)
        n_used = tile_end[-1]
        tile_bucket = jnp.sum((tile_end[None, :] <= jnp.minimum(tile_ids, n_used - 1)[:, None]).astype(I32), axis=1)
        tile_bucket = jnp.minimum(tile_bucket, N_BUCKETS - 1)
        tile_lo = jnp.asarray(pair_lo)[tile_bucket]
        tile_hi = jnp.asarray(pair_hi)[tile_bucket]
        tile_valid = (tile_ids < n_used).astype(I32)

        xs = _dispatch(route, seg_start, hm, jnp.zeros((R, D), F32), tm_row)
        ys = _experts(tile_lo, tile_hi, tile_valid, xs, w1[l].astype(BF16), w3[l].astype(BF16),
                      w2[l].astype(BF16), tr)
        x2 = _combine(route, seg_start, x1, rw, ys, tm_row)
    return x2.reshape(B, S, D)
```

```python
import functools
import math

import numpy as np
import jax
import jax.numpy as jnp
from jax import lax
from jax.experimental import pallas as pl
from jax.experimental.pallas import tpu as pltpu

F32 = jnp.float32
BF16 = jnp.bfloat16
I32 = jnp.int32

D_MODEL = 1024
EPS = 1e-6

DA_HEADS = 8
DA_HEAD_DIM = 64
DA_V_HEAD_DIM = 128

RET_HEADS = 4
RET_KEY_DIM = 256
RET_VAL_DIM = 512
RET_CHUNK = 128
RET_ROT_BASE = 10000.0

SEG = 1024
IN_WIDTH = 11 * SEG
J_DA_Q, J_DA_K, J_DA_V, J_RET_Q, J_RET_K = 0, 1, 2, 3, 4
J_RET_G0, J_RET_G1, J_GATE_A, J_GATE_B = 7, 8, 9, 10

N_GROUPS = 4
EXPERTS_PER_GROUP = 8
N_EXPERTS = 32
D_FF = 512
PAIRS_PER_GROUP = 28
N_BUCKETS = N_GROUPS * PAIRS_PER_GROUP
ROUTE_LANES = 128
RANK_BITS = 17

NEG_BIG = -1e30
LOG2E = 1.4426950408889634

VMEM_LIMIT = 56 * 1024 * 1024


def _cparams(sem):
    return pltpu.CompilerParams(dimension_semantics=sem, vmem_limit_bytes=VMEM_LIMIT)


def _proj_kernel(x_ref, nw_ref, w_ref, cos_ref, sine_ref, sino_ref, qnw_ref, knw_ref, gsum_ref, o_ref, h_scr,
                 *, strip):
    j = pl.program_id(1)
    tm = x_ref.shape[0]

    @pl.when(j == 0)
    def _():
        x = x_ref[...]
        r = lax.rsqrt(jnp.mean(x * x, axis=-1, keepdims=True) + EPS)
        h_scr[...] = ((x * r) * nw_ref[...]).astype(BF16)

    def run(epilogue):
        for s in range(tm // strip):
            rows = slice(s * strip, (s + 1) * strip)
            y = jnp.dot(h_scr[rows, :], w_ref[...], preferred_element_type=F32)
            epilogue(y, rows)

    def qk_norm(w_ref_, scale):
        def epilogue(y, rows):
            for c in range(SEG // 256):
                sl = slice(c * 256, (c + 1) * 256)
                yc = y[:, sl]
                ms = jnp.dot((yc * yc).astype(BF16), gsum_ref[...], preferred_element_type=F32)
                o_ref[rows, sl] = ((yc * lax.rsqrt(ms + EPS)) * (w_ref_[:, sl] * scale)).astype(o_ref.dtype)
        return epilogue

    def rotate(scale):
        def epilogue(y, rows):
            for c in range(SEG // 128):
                sl = slice(c * 128, (c + 1) * 128)
                tl = slice((c % 2) * 128, (c % 2 + 1) * 128)
                t = y[:, sl]
                t_next = pltpu.roll(t, 127, 1)
                t_prev = pltpu.roll(t, 1, 1)
                out = t * cos_ref[rows, tl] + t_next * sine_ref[rows, tl] + t_prev * sino_ref[rows, tl]
                o_ref[rows, sl] = (out * scale).astype(o_ref.dtype)
        return epilogue

    def pointwise(fn):
        def epilogue(y, rows):
            o_ref[rows, :] = fn(y).astype(o_ref.dtype)
        return epilogue

    @pl.when(j == J_DA_Q)
    def _():
        run(qk_norm(qnw_ref, DA_HEAD_DIM ** -0.5 * LOG2E))

    @pl.when(j == J_DA_K)
    def _():
        run(qk_norm(knw_ref, 1.0))

    @pl.when(j == J_RET_Q)
    def _():
        run(rotate(1.0))

    @pl.when(j == J_RET_K)
    def _():
        run(rotate(RET_KEY_DIM ** -0.5))

    @pl.when((j == J_DA_V) | (j == 5) | (j == 6))
    def _():
        run(pointwise(lambda y: y))

    @pl.when((j == J_RET_G0) | (j == J_RET_G1))
    def _():
        run(pointwise(lambda y: y * jax.nn.sigmoid(y)))

    @pl.when((j == J_GATE_A) | (j == J_GATE_B))
    def _():
        run(pointwise(jax.nn.sigmoid))


def _proj(x2, attn_norm_w, w_in_bf, cos_t, sine_t, sino_t, qnw_t, knw_t, gsum, seq, tm):
    T = x2.shape[0]
    n_pos_blocks = seq // tm
    return pl.pallas_call(
        functools.partial(_proj_kernel, strip=min(256, tm)),
        grid=(T // tm, IN_WIDTH // SEG),
        in_specs=[
            pl.BlockSpec((tm, D_MODEL), lambda i, j: (i, 0)),
            pl.BlockSpec((1, D_MODEL), lambda i, j: (0, 0)),
            pl.BlockSpec((D_MODEL, SEG), lambda i, j: (0, j)),
            pl.BlockSpec((tm, RET_KEY_DIM), lambda i, j: (i % n_pos_blocks, 0)),
            pl.BlockSpec((tm, RET_KEY_DIM), lambda i, j: (i % n_pos_blocks, 0)),
            pl.BlockSpec((tm, RET_KEY_DIM), lambda i, j: (i % n_pos_blocks, 0)),
            pl.BlockSpec((1, SEG), lambda i, j: (0, 0)),
            pl.BlockSpec((1, SEG), lambda i, j: (0, 0)),
            pl.BlockSpec((256, 256), lambda i, j: (0, 0)),
        ],
        out_specs=pl.BlockSpec((tm, SEG), lambda i, j: (i, j)),
        out_shape=jax.ShapeDtypeStruct((T, IN_WIDTH), BF16),
        scratch_shapes=[pltpu.VMEM((tm, D_MODEL), BF16)],
        compiler_params=_cparams(("arbitrary", "arbitrary")),
        name="proj",
    )(x2, attn_norm_w, w_in_bf, cos_t, sine_t, sino_t, qnw_t, knw_t, gsum)


def _attn_kernel(q_ref, k_ref, v_ref, lam_ref, subw_ref, o_ref, q2_scr, s_scr, m_scr, acc_scr, *,
                 lam_init, strip):
    qi = pl.program_id(2)
    tq = q_ref.shape[0]
    tk = tq
    dv = DA_V_HEAD_DIM

    q = q_ref[...]
    lane = lax.broadcasted_iota(I32, q.shape, 1)
    zero = jnp.zeros_like(q)
    q2_scr[0:tq, :] = jnp.where(lane < DA_HEAD_DIM, q, zero)
    q2_scr[tq:2 * tq, :] = jnp.where(lane >= DA_HEAD_DIM, q, zero)
    m_scr[...] = jnp.full(m_scr.shape, NEG_BIG, F32)
    acc_scr[...] = jnp.zeros(acc_scr.shape, F32)
    ones = jnp.ones((tk, dv), BF16)

    def scores(j, slot):
        off = pl.multiple_of(j * tk, tk)
        s_scr[slot] = lax.dot_general(q2_scr[...], k_ref[pl.ds(off, tk), :], (((1,), (1,)), ((), ())),
                                      preferred_element_type=F32)

    def softmax_pv(j, slot, masked):
        off = pl.multiple_of(j * tk, tk)
        vext = jnp.concatenate([v_ref[pl.ds(off, tk), :], ones], axis=1)
        for r in range(2 * tq // strip):
            rows = slice(r * strip, (r + 1) * strip)
            s = s_scr[slot, rows, :]
            if masked:
                qrow = (r * strip) % tq + lax.broadcasted_iota(I32, (strip, tk), 0)
                col = lax.broadcasted_iota(I32, (strip, tk), 1)
                s = jnp.where(col > qrow, NEG_BIG, s)
            m_prev = m_scr[rows, :]
            m_new = jnp.maximum(m_prev, jnp.max(s, axis=1, keepdims=True))
            alpha = jnp.exp2(m_prev - m_new)
            p = jnp.exp2(s - jnp.concatenate([m_new] * (tk // 128), axis=1))
            pv = jnp.dot(p.astype(BF16), vext, preferred_element_type=F32)
            acc_scr[rows, :] = jnp.concatenate([alpha, alpha], axis=1) * acc_scr[rows, :] + pv
            m_scr[rows, :] = m_new

    scores(0, 0)
    n_pairs = qi // 2

    def body(i, carry):
        j = 2 * i
        scores(j + 1, 1)
        softmax_pv(j, 0, False)
        scores(j + 2, 0)
        softmax_pv(j + 1, 1, False)
        return carry

    lax.fori_loop(0, n_pairs, body, 0)

    @pl.when(qi % 2 == 1)
    def _():
        scores(qi, 1)
        softmax_pv(qi - 1, 0, False)
        softmax_pv(qi, 1, True)

    @pl.when(qi % 2 == 0)
    def _():
        softmax_pv(qi, 0, True)

    o1 = acc_scr[0:tq, 0:dv] / acc_scr[0:tq, dv:2 * dv]
    o2 = acc_scr[tq:2 * tq, 0:dv] / acc_scr[tq:2 * tq, dv:2 * dv]
    lam_p = lam_ref[...]
    lam = (jnp.exp(jnp.sum(lam_p[0:1, :] * lam_p[1:2, :], axis=1, keepdims=True))
           - jnp.exp(jnp.sum(lam_p[2:3, :] * lam_p[3:4, :], axis=1, keepdims=True)) + lam_init)
    o = o1 - lam * o2
    r = lax.rsqrt(jnp.mean(o * o, axis=-1, keepdims=True) + EPS)
    o_ref[...] = ((o * r) * subw_ref[...] * (1.0 - lam_init)).astype(o_ref.dtype)


def _attn(P, lam_p, subw, batch, seq, tq, lam_init):
    T = P.shape[0]
    nq = seq // tq
    q_col = J_DA_Q * (SEG // 128)
    k_col = J_DA_K * (SEG // 128)
    v_col = J_DA_V * (SEG // 128)
    return pl.pallas_call(
        functools.partial(_attn_kernel, lam_init=lam_init, strip=min(128, tq)),
        grid=(batch, DA_HEADS, nq),
        in_specs=[
            pl.BlockSpec((tq, 128), lambda b, h, qi: (b * nq + qi, q_col + h)),
            pl.BlockSpec((seq, 128), lambda b, h, qi: (b, k_col + h)),
            pl.BlockSpec((seq, 128), lambda b, h, qi: (b, v_col + h)),
            pl.BlockSpec((4, DA_HEAD_DIM), lambda b, h, qi: (0, 0)),
            pl.BlockSpec((1, DA_V_HEAD_DIM), lambda b, h, qi: (0, 0)),
        ],
        out_specs=pl.BlockSpec((tq, 128), lambda b, h, qi: (b * nq + qi, h)),
        out_shape=jax.ShapeDtypeStruct((T, DA_HEADS * DA_V_HEAD_DIM), BF16),
        scratch_shapes=[
            pltpu.VMEM((2 * tq, 128), BF16),
            pltpu.VMEM((2, 2 * tq, tq), F32),
            pltpu.VMEM((2 * tq, 128), F32),
            pltpu.VMEM((2 * tq, 2 * DA_V_HEAD_DIM), F32),
        ],
        compiler_params=_cparams(("arbitrary", "arbitrary", "arbitrary")),
        name="attn",
    )(P, P, P, lam_p, subw)


def _ret_kernel(q_ref, k_ref, va_ref, vb_ref, ga_ref, gb_ref, dec_ref, xi_ref, zeta_ref, cd_ref, rw_ref,
                o_ref, st_scr):
    n = pl.program_id(1)
    ts = q_ref.shape[0]
    C = RET_CHUNK
    dk, dv = RET_KEY_DIM, RET_VAL_DIM
    heads_per_tile = SEG // dv

    @pl.when(n == 0)
    def _():
        st_scr[...] = jnp.zeros(st_scr.shape, F32)

    for c in range(ts // C):
        rows = slice(c * C, (c + 1) * C)
        for h in range(RET_HEADS):
            kcols = slice(h * dk, (h + 1) * dk)
            v_ref, g_ref = (va_ref, ga_ref) if h < heads_per_tile else (vb_ref, gb_ref)
            vcols = slice((h % heads_per_tile) * dv, (h % heads_per_tile + 1) * dv)
            qc = q_ref[rows, kcols]
            kc = k_ref[rows, kcols]
            vc = v_ref[rows, vcols]
            inner = lax.dot_general(qc, kc, (((1,), (1,)), ((), ())), preferred_element_type=F32) * dec_ref[h]
            st = st_scr[h]
            qx = (qc.astype(F32) * xi_ref[h]).astype(BF16)
            o = (jnp.dot(inner.astype(BF16), vc, preferred_element_type=F32)
                 + jnp.dot(qx, st.astype(BF16), preferred_element_type=F32))
            kz = (kc.astype(F32) * zeta_ref[h]).astype(BF16)
            st_scr[h] = cd_ref[h] * st + lax.dot_general(kz, vc, (((0,), (0,)), ((), ())),
                                                         preferred_element_type=F32)
            r = lax.rsqrt(jnp.mean(o * o, axis=-1, keepdims=True) + EPS)
            o_ref[rows, h * dv:(h + 1) * dv] = (
                g_ref[rows, vcols].astype(F32) * ((o * r) * rw_ref[...])).astype(o_ref.dtype)


def _ret(P, decay, xi, zeta, cd, rw, batch, seq, ts):
    T = P.shape[0]
    ns = seq // ts
    C = RET_CHUNK
    tile = lambda j: pl.BlockSpec((ts, SEG), lambda b, n: (b * ns + n, j))
    full = lambda shape: pl.BlockSpec(shape, lambda b, n: (0,) * len(shape))
    return pl.pallas_call(
        _ret_kernel,
        grid=(batch, ns),
        in_specs=[
            tile(J_RET_Q), tile(J_RET_K), tile(5), tile(6), tile(J_RET_G0), tile(J_RET_G1),
            full((RET_HEADS, C, C)),
            full((RET_HEADS, C, 1)),
            full((RET_HEADS, C, 1)),
            full((RET_HEADS, 1, 1)),
            full((1, RET_VAL_DIM)),
        ],
        out_specs=pl.BlockSpec((ts, RET_HEADS * RET_VAL_DIM), lambda b, n: (b * ns + n, 0)),
        out_shape=jax.ShapeDtypeStruct((T, RET_HEADS * RET_VAL_DIM), BF16),
        scratch_shapes=[pltpu.VMEM((RET_HEADS, RET_KEY_DIM, RET_VAL_DIM), F32)],
        compiler_params=_cparams(("arbitrary", "arbitrary")),
        name="ret",
    )(P, P, P, P, P, P, decay, xi, zeta, cd, rw)


def _merge_kernel(x_ref, oa_ref, rb_ref, ga_ref, gb_ref, wa_ref, wb_ref, wo_ref, mnw_ref,
                  wrh_ref, wrl_ref, br_ref, ltri_ref,
                  x1_ref, hm_ref, ri_ref, rw_ref, cnt_ref, carry_scr, *, strip):
    i = pl.program_id(0)
    tm = x_ref.shape[0]

    @pl.when(i == 0)
    def _():
        carry_scr[...] = jnp.zeros(carry_scr.shape, F32)

    carry = carry_scr[...]
    for s in range(tm // strip):
        rows = slice(s * strip, (s + 1) * strip)
        ya = jnp.dot(oa_ref[rows, :], wa_ref[...], preferred_element_type=F32)
        yb = jnp.dot(rb_ref[rows, :], wb_ref[...], preferred_element_type=F32)
        z = ga_ref[rows, :].astype(F32) * ya + gb_ref[rows, :].astype(F32) * yb
        x1 = x_ref[rows, :] + jnp.dot(z.astype(BF16), wo_ref[...], preferred_element_type=F32)
        x1_ref[rows, :] = x1
        r = lax.rsqrt(jnp.mean(x1 * x1, axis=-1, keepdims=True) + EPS)
        hm = (x1 * r) * mnw_ref[...]
        hm_ref[rows, :] = hm

        hm_hi = hm.astype(BF16)
        hm_lo = (hm - hm_hi.astype(F32)).astype(BF16)
        logits = (jnp.dot(hm_hi, wrh_ref[...], preferred_element_type=F32)
                  + jnp.dot(hm_lo, wrh_ref[...], preferred_element_type=F32)
                  + jnp.dot(hm_hi, wrl_ref[...], preferred_element_type=F32)) + br_ref[...]

        lane = lax.broadcasted_iota(I32, (strip, ROUTE_LANES), 1)
        lane_f = lane.astype(F32)

        def first_argmax(vals, vmax):
            return jnp.min(jnp.where(vals == vmax, lane_f, float(ROUTE_LANES)), axis=1, keepdims=True).astype(I32)

        gmask = lane < N_GROUPS
        lg = jnp.where(gmask, logits, NEG_BIG)
        mg = jnp.max(lg, axis=1, keepdims=True)
        g_idx = first_argmax(lg, mg)
        p_g = 1.0 / jnp.sum(jnp.where(gmask, jnp.exp(lg - mg), 0.0), axis=1, keepdims=True)

        lo_lane = N_GROUPS + EXPERTS_PER_GROUP * g_idx
        emask = (lane >= lo_lane) & (lane < lo_lane + EXPERTS_PER_GROUP)
        le = jnp.where(emask, logits, NEG_BIG)
        m1 = jnp.max(le, axis=1, keepdims=True)
        i1 = first_argmax(le, m1)
        le2 = jnp.where(lane == i1, NEG_BIG, le)
        m2 = jnp.max(le2, axis=1, keepdims=True)
        i2 = first_argmax(le2, m2)
        e21 = jnp.exp(m2 - m1)
        w_top1 = p_g / (1.0 + e21)
        w_top2 = p_g * e21 / (1.0 + e21)

        a1 = i1 - lo_lane
        a2 = i2 - lo_lane
        first_is_lo = a1 < a2
        a_lo = jnp.where(first_is_lo, a1, a2)
        a_hi = jnp.where(first_is_lo, a2, a1)
        w_lo = jnp.where(first_is_lo, w_top1, w_top2)
        w_hi = jnp.where(first_is_lo, w_top2, w_top1)
        pair = ((a_lo * (2 * EXPERTS_PER_GROUP - 1 - a_lo)) >> 1) + (a_hi - a_lo - 1)
        bucket = g_idx * PAIRS_PER_GROUP + pair

        oh = jnp.where(lane == bucket, 1.0, 0.0)
        prefix = jnp.dot(ltri_ref[...], oh.astype(BF16), preferred_element_type=F32) + carry
        rank = jnp.sum(jnp.where(lane == bucket, prefix, 0.0), axis=1, keepdims=True).astype(I32)
        carry = carry + jnp.sum(oh, axis=0, keepdims=True)

        zero_i = jnp.zeros((strip, ROUTE_LANES), I32)
        ri_ref[rows, :] = jnp.where(lane == 0, (bucket << RANK_BITS) | rank, zero_i)
        rw_ref[rows, :] = jnp.where(lane == 0, w_lo, jnp.where(lane == 1, w_hi, 0.0))
    carry_scr[...] = carry
    cnt_ref[...] = carry


def _merge(x2, oa, rb, P, wa, wb, wo, mnw, wr_hi, wr_lo, br, ltri, tm):
    T = x2.shape[0]
    full = lambda shape: pl.BlockSpec(shape, lambda i: (0,) * len(shape))
    return pl.pallas_call(
        functools.partial(_merge_kernel, strip=ltri.shape[0]),
        grid=(T // tm,),
        in_specs=[
            pl.BlockSpec((tm, D_MODEL), lambda i: (i, 0)),
            pl.BlockSpec((tm, D_MODEL), lambda i: (i, 0)),
            pl.BlockSpec((tm, 2 * D_MODEL), lambda i: (i, 0)),
            pl.BlockSpec((tm, SEG), lambda i: (i, J_GATE_A)),
            pl.BlockSpec((tm, SEG), lambda i: (i, J_GATE_B)),
            full((D_MODEL, D_MODEL)),
            full((2 * D_MODEL, D_MODEL)),
            full((D_MODEL, D_MODEL)),
            full((1, D_MODEL)),
            full((D_MODEL, ROUTE_LANES)),
            full((D_MODEL, ROUTE_LANES)),
            full((1, ROUTE_LANES)),
            full(ltri.shape),
        ],
        out_specs=[
            pl.BlockSpec((tm, D_MODEL), lambda i: (i, 0)),
            pl.BlockSpec((tm, D_MODEL), lambda i: (i, 0)),
            pl.BlockSpec((tm, ROUTE_LANES), lambda i: (i, 0)),
            pl.BlockSpec((tm, ROUTE_LANES), lambda i: (i, 0)),
            full((1, ROUTE_LANES)),
        ],
        out_shape=[
            jax.ShapeDtypeStruct((T, D_MODEL), F32),
            jax.ShapeDtypeStruct((T, D_MODEL), F32),
            jax.ShapeDtypeStruct((T, ROUTE_LANES), I32),
            jax.ShapeDtypeStruct((T, ROUTE_LANES), F32),
            jax.ShapeDtypeStruct((1, ROUTE_LANES), F32),
        ],
        scratch_shapes=[pltpu.VMEM((1, ROUTE_LANES), F32)],
        compiler_params=_cparams(("arbitrary",)),
        name="merge",
    )(x2, oa, rb, P, P, wa, wb, wo, mnw, wr_hi, wr_lo, br, ltri)


def _sorted_row(route_ref, seg_ref, t):
    word = route_ref[t]
    return seg_ref[word >> RANK_BITS] + (word & ((1 << RANK_BITS) - 1))


def _dispatch_kernel(route_ref, seg_ref, hm_ref, xs_in_ref, xs_ref, sem):
    del xs_in_ref
    i = pl.program_id(0)
    tm = hm_ref.shape[0]
    base = i * tm

    def row_copy(r):
        d = _sorted_row(route_ref, seg_ref, base + r)
        return pltpu.make_async_copy(hm_ref.at[pl.ds(r, 1), :], xs_ref.at[pl.ds(d, 1), :], sem)

    for r in range(tm):
        row_copy(r).start()
    for r in range(tm):
        row_copy(r).wait()


def _dispatch(route, seg_start, hm, xs_zero, tm):
    T = hm.shape[0]
    return pl.pallas_call(
        _dispatch_kernel,
        grid_spec=pltpu.PrefetchScalarGridSpec(
            num_scalar_prefetch=2,
            grid=(T // tm,),
            in_specs=[
                pl.BlockSpec((tm, D_MODEL), lambda i, route, seg: (i, 0)),
                pl.BlockSpec(memory_space=pl.ANY),
            ],
            out_specs=pl.BlockSpec(memory_space=pl.ANY),
            scratch_shapes=[pltpu.SemaphoreType.DMA(())],
        ),
        out_shape=jax.ShapeDtypeStruct(xs_zero.shape, xs_zero.dtype),
        input_output_aliases={3: 0},
        compiler_params=_cparams(("arbitrary",)),
        name="dispatch",
    )(route, seg_start, hm, xs_zero)


def _experts_kernel(tlo_ref, thi_ref, tvalid_ref, x_ref, w1l_ref, w3l_ref, w2l_ref,
                    w1h_ref, w3h_ref, w2h_ref, o_ref):
    del tlo_ref, thi_ref
    i = pl.program_id(0)

    @pl.when(tvalid_ref[i] == 1)
    def _():
        x = x_ref[...].astype(BF16)

        def mlp(w1_ref, w3_ref, w2_ref):
            a = jnp.dot(x, w1_ref[0], preferred_element_type=F32)
            b = jnp.dot(x, w3_ref[0], preferred_element_type=F32)
            h = (a * jax.nn.sigmoid(a)) * b
            return jnp.dot(h.astype(BF16), w2_ref[0], preferred_element_type=F32)

        o_ref[:, 0:D_MODEL] = mlp(w1l_ref, w3l_ref, w2l_ref)
        o_ref[:, D_MODEL:2 * D_MODEL] = mlp(w1h_ref, w3h_ref, w2h_ref)

    @pl.when(tvalid_ref[i] == 0)
    def _():
        o_ref[...] = jnp.zeros(o_ref.shape, o_ref.dtype)


def _experts(tile_lo, tile_hi, tile_valid, xs, w1, w3, w2, tr):
    R = xs.shape[0]
    w_up = lambda which: pl.BlockSpec(
        (1, D_MODEL, D_FF), lambda i, tlo, thi, tv: ((tlo if which == 0 else thi)[i], 0, 0))
    w_dn = lambda which: pl.BlockSpec(
        (1, D_FF, D_MODEL), lambda i, tlo, thi, tv: ((tlo if which == 0 else thi)[i], 0, 0))
    return pl.pallas_call(
        _experts_kernel,
        grid_spec=pltpu.PrefetchScalarGridSpec(
            num_scalar_prefetch=3,
            grid=(R // tr,),
            in_specs=[
                pl.BlockSpec((tr, D_MODEL), lambda i, tlo, thi, tv: (i, 0)),
                w_up(0), w_up(0), w_dn(0), w_up(1), w_up(1), w_dn(1),
            ],
            out_specs=pl.BlockSpec((tr, 2 * D_MODEL), lambda i, tlo, thi, tv: (i, 0)),
        ),
        out_shape=jax.ShapeDtypeStruct((R, 2 * D_MODEL), F32),
        compiler_params=_cparams(("arbitrary",)),
        name="experts",
    )(tile_lo, tile_hi, tile_valid, xs, w1, w3, w2, w1, w3, w2)


def _combine_kernel(route_ref, seg_ref, x1_ref, rw_ref, ys_ref, o_ref, buf, sem):
    i = pl.program_id(0)
    n_blocks = pl.num_programs(0) - 1
    tm = x1_ref.shape[0]

    def row_copy(block, slot, r):
        d = _sorted_row(route_ref, seg_ref, block * tm + r)
        return pltpu.make_async_copy(ys_ref.at[pl.ds(d, 1), :], buf.at[slot, pl.ds(r, 1), :], sem.at[slot])

    @pl.when(i < n_blocks)
    def _():
        for r in range(tm):
            row_copy(i, i % 2, r).start()

    @pl.when(i > 0)
    def _():
        slot = (i - 1) % 2
        for r in range(tm):
            row_copy(i - 1, slot, r).wait()
        rw = rw_ref[...]
        o_ref[...] = (x1_ref[...] + rw[:, 0:1] * buf[slot, :, 0:D_MODEL]
                      + rw[:, 1:2] * buf[slot, :, D_MODEL:2 * D_MODEL])


def _combine(route, seg_start, x1, rw, ys, tm):
    T = x1.shape[0]
    return pl.pallas_call(
        _combine_kernel,
        grid_spec=pltpu.PrefetchScalarGridSpec(
            num_scalar_prefetch=2,
            grid=(T // tm + 1,),
            in_specs=[
                pl.BlockSpec((tm, D_MODEL), lambda i, route, seg: (jnp.maximum(i - 1, 0), 0)),
                pl.BlockSpec((tm, ROUTE_LANES), lambda i, route, seg: (jnp.maximum(i - 1, 0), 0)),
                pl.BlockSpec(memory_space=pl.ANY),
            ],
            out_specs=pl.BlockSpec((tm, D_MODEL), lambda i, route, seg: (jnp.maximum(i - 1, 0), 0)),
            scratch_shapes=[pltpu.VMEM((2, tm, 2 * D_MODEL), F32), pltpu.SemaphoreType.DMA((2,))],
        ),
        out_shape=jax.ShapeDtypeStruct((T, D_MODEL), F32),
        compiler_params=_cparams(("arbitrary",)),
        name="combine",
    )(route, seg_start, x1, rw, ys)


def _rotation_tables(seq):
    angle = 1.0 / (RET_ROT_BASE ** jnp.linspace(0.0, 1.0, RET_KEY_DIM // 2, dtype=F32))
    angle = jnp.repeat(angle, 2)
    ang = jnp.arange(seq).astype(F32)[:, None] * angle[None, :]
    sin = jnp.sin(ang)
    even = (jnp.arange(RET_KEY_DIM) % 2 == 0)[None, :]
    return jnp.cos(ang), jnp.where(even, -sin, 0.0), jnp.where(even, 0.0, sin)


def _retention_tables():
    C = RET_CHUNK
    log_g = jnp.log1p(-jnp.exp2(-5.0 - jnp.arange(RET_HEADS, dtype=F32)))
    j = jnp.arange(C, dtype=F32)
    rel = j[:, None] - j[None, :]
    decay = jnp.where(rel >= 0, jnp.exp(log_g[:, None, None] * jnp.maximum(rel, 0.0)), 0.0)
    xi = jnp.exp(log_g[:, None] * (j + 1.0))[:, :, None]
    zeta = jnp.exp(log_g[:, None] * (C - 1.0 - j))[:, :, None]
    cd = jnp.exp(log_g * C)[:, None, None]
    return decay, xi, zeta, cd


def _pair_tables():
    lo, hi = [], []
    for g in range(N_GROUPS):
        for a in range(EXPERTS_PER_GROUP):
            for b in range(a + 1, EXPERTS_PER_GROUP):
                lo.append(g * EXPERTS_PER_GROUP + a)
                hi.append(g * EXPERTS_PER_GROUP + b)
    return np.asarray(lo, np.int32), np.asarray(hi, np.int32)


def _pick(pref, n):
    t = min(pref, n)
    while n % t:
        t //= 2
    return t


def kernel(x, attn_norm_w, w_in, q_norm_w, k_norm_w, lambda_q1, lambda_k1, lambda_q2, lambda_k2,
           da_subln_w, ret_norm_w, w_branch_a, w_branch_b, w_out, moe_norm_w,
           w_group_router, b_group_router, w_expert_router, b_expert_router, w1, w3, w2):
    B, S, D = x.shape
    T = B * S
    assert T <= (1 << RANK_BITS)
    depth = attn_norm_w.shape[0]

    tm_proj = _pick(1024, S)
    tq = _pick(512, S)
    ts_ret = _pick(512, S)
    tm_merge = _pick(512, T)
    tm_row = _pick(256, T)
    tr = 256

    cos_t, sine_t, sino_t = _rotation_tables(S)
    decay, xi, zeta, cd = _retention_tables()
    gidx = np.arange(256) // DA_HEAD_DIM
    gsum = jnp.asarray((gidx[:, None] == gidx[None, :]).astype(np.float32) / DA_HEAD_DIM, BF16)
    merge_strip = tm_merge
    ltri = jnp.asarray(np.tril(np.ones((merge_strip, merge_strip), np.float32), -1), BF16)
    pair_lo, pair_hi = _pair_tables()
    n_tiles = T // tr + N_BUCKETS
    R = n_tiles * tr

    x2 = x.reshape(T, D)
    for l in range(depth):
        lam_init = 0.8 - 0.6 * math.exp(-0.3 * l)
        P = _proj(x2, attn_norm_w[l][None, :], w_in[l].astype(BF16), cos_t, sine_t, sino_t,
                  jnp.tile(q_norm_w[l], SEG // DA_HEAD_DIM)[None, :],
                  jnp.tile(k_norm_w[l], SEG // DA_HEAD_DIM)[None, :], gsum, S, tm_proj)
        lam_p = jnp.stack([lambda_q1[l], lambda_k1[l], lambda_q2[l], lambda_k2[l]]).astype(F32)
        oa = _attn(P, lam_p, da_subln_w[l][None, :], B, S, tq, lam_init)
        rb = _ret(P, decay, xi, zeta, cd, ret_norm_w[l][None, :].astype(F32), B, S, ts_ret)

        w_router = jnp.zeros((D, ROUTE_LANES), F32)
        w_router = w_router.at[:, 0:N_GROUPS].set(w_group_router[l])
        w_router = w_router.at[:, N_GROUPS:N_GROUPS + N_EXPERTS].set(w_expert_router[l])
        wr_hi = w_router.astype(BF16)
        wr_lo = (w_router - wr_hi.astype(F32)).astype(BF16)
        b_router = jnp.zeros((1, ROUTE_LANES), F32)
        b_router = b_router.at[0, 0:N_GROUPS].set(b_group_router[l])
        b_router = b_router.at[0, N_GROUPS:N_GROUPS + N_EXPERTS].set(b_expert_router[l].reshape(-1))

        x1, hm, ri, rw, cnt = _merge(
            x2, oa, rb, P, w_branch_a[l].astype(BF16), w_branch_b[l].astype(BF16),
            w_out[l].astype(BF16), moe_norm_w[l][None, :], wr_hi, wr_lo, b_router, ltri, tm_merge)

        counts = cnt[0, 0:N_BUCKETS].astype(I32)
        tiles_per = (counts + tr - 1) // tr
        tile_end = jnp.cumsum(tiles_per)
        seg_start = (tile_end - tiles_per) * tr
        route = ri[:, 0]
        tile_ids = jnp.arange(n_tiles, dtype=I32)
        n_used = tile_end[-1]
        tile_bucket = jnp.sum((tile_end[None, :] <= jnp.minimum(tile_ids, n_used - 1)[:, None]).astype(I32), axis=1)
        tile_bucket = jnp.minimum(tile_bucket, N_BUCKETS - 1)
        tile_lo = jnp.asarray(pair_lo)[tile_bucket]
        tile_hi = jnp.asarray(pair_hi)[tile_bucket]
        tile_valid = (tile_ids < n_used).astype(I32)

        xs = _dispatch(route, seg_start, hm, jnp.zeros((R, D), F32), tm_row)
        ys = _experts(tile_lo, tile_hi, tile_valid, xs, w1[l].astype(BF16), w3[l].astype(BF16),
                      w2[l].astype(BF16), tr)
        x2 = _combine(route, seg_start, x1, rw, ys, tm_row)
    return x2.reshape(B, S, D)
```

```python
import functools
import math

import numpy as np
import jax
import jax.numpy as jnp
from jax import lax
from jax.experimental import pallas as pl
from jax.experimental.pallas import tpu as pltpu

F32 = jnp.float32
BF16 = jnp.bfloat16
I32 = jnp.int32

D_MODEL = 1024
EPS = 1e-6

DA_HEADS = 8
DA_HEAD_DIM = 64
DA_V_HEAD_DIM = 128

RET_HEADS = 4
RET_KEY_DIM = 256
RET_VAL_DIM = 512
RET_CHUNK = 128
RET_ROT_BASE = 10000.0

SEG = 1024
IN_WIDTH = 11 * SEG
J_DA_Q, J_DA_K, J_DA_V, J_RET_Q, J_RET_K = 0, 1, 2, 3, 4
J_RET_G0, J_RET_G1, J_GATE_A, J_GATE_B = 7, 8, 9, 10

N_GROUPS = 4
EXPERTS_PER_GROUP = 8
N_EXPERTS = 32
D_FF = 512
PAIRS_PER_GROUP = 28
N_BUCKETS = N_GROUPS * PAIRS_PER_GROUP
ROUTE_LANES = 128
RANK_BITS = 17

NEG_BIG = -1e30
LOG2E = 1.4426950408889634

VMEM_LIMIT = 56 * 1024 * 1024


def _cparams(sem):
    return pltpu.CompilerParams(dimension_semantics=sem, vmem_limit_bytes=VMEM_LIMIT)


U32 = jnp.uint32
HI_HALF = np.uint32(0xFFFF0000)


def _pack_bf16_pair(lo, hi):
    lo_bits = lax.bitcast_convert_type(lo.astype(BF16).astype(F32), U32)
    hi_bits = lax.bitcast_convert_type(hi.astype(BF16).astype(F32), U32)
    return (lo_bits >> 16) | (hi_bits & HI_HALF)


def _unpack_bf16_pair(word):
    lo = lax.bitcast_convert_type(word << 16, F32)
    hi = lax.bitcast_convert_type(word & HI_HALF, F32)
    return lo, hi


def _proj_kernel(x_ref, nw_ref, w_ref, cos_ref, sine_ref, sino_ref, qnw_ref, knw_ref, gsum_ref, o_ref, h_scr,
                 *, strip):
    j = pl.program_id(1)
    tm = x_ref.shape[0]

    @pl.when(j == 0)
    def _():
        x = x_ref[...]
        r = lax.rsqrt(jnp.mean(x * x, axis=-1, keepdims=True) + EPS)
        h_scr[...] = ((x * r) * nw_ref[...]).astype(BF16)

    def run(epilogue):
        for s in range(tm // strip):
            rows = slice(s * strip, (s + 1) * strip)
            y = jnp.dot(h_scr[rows, :], w_ref[...], preferred_element_type=F32)
            epilogue(y, rows)

    def qk_norm(w_ref_, scale):
        def epilogue(y, rows):
            for c in range(SEG // 256):
                sl = slice(c * 256, (c + 1) * 256)
                yc = y[:, sl]
                ms = jnp.dot((yc * yc).astype(BF16), gsum_ref[...], preferred_element_type=F32)
                o_ref[rows, sl] = ((yc * lax.rsqrt(ms + EPS)) * (w_ref_[:, sl] * scale)).astype(o_ref.dtype)
        return epilogue

    def rotate(scale):
        def epilogue(y, rows):
            for c in range(SEG // 128):
                sl = slice(c * 128, (c + 1) * 128)
                tl = slice((c % 2) * 128, (c % 2 + 1) * 128)
                t = y[:, sl]
                t_next = pltpu.roll(t, 127, 1)
                t_prev = pltpu.roll(t, 1, 1)
                out = t * cos_ref[rows, tl] + t_next * sine_ref[rows, tl] + t_prev * sino_ref[rows, tl]
                o_ref[rows, sl] = (out * scale).astype(o_ref.dtype)
        return epilogue

    def pointwise(fn):
        def epilogue(y, rows):
            o_ref[rows, :] = fn(y).astype(o_ref.dtype)
        return epilogue

    @pl.when(j == J_DA_Q)
    def _():
        run(qk_norm(qnw_ref, DA_HEAD_DIM ** -0.5 * LOG2E))

    @pl.when(j == J_DA_K)
    def _():
        run(qk_norm(knw_ref, 1.0))

    @pl.when(j == J_RET_Q)
    def _():
        run(rotate(1.0))

    @pl.when(j == J_RET_K)
    def _():
        run(rotate(RET_KEY_DIM ** -0.5))

    @pl.when((j == J_DA_V) | (j == 5) | (j == 6))
    def _():
        run(pointwise(lambda y: y))

    @pl.when((j == J_RET_G0) | (j == J_RET_G1))
    def _():
        run(pointwise(lambda y: y * jax.nn.sigmoid(y)))

    @pl.when((j == J_GATE_A) | (j == J_GATE_B))
    def _():
        run(pointwise(jax.nn.sigmoid))


def _proj(x2, attn_norm_w, w_in_bf, cos_t, sine_t, sino_t, qnw_t, knw_t, gsum, seq, tm):
    T = x2.shape[0]
    n_pos_blocks = seq // tm
    return pl.pallas_call(
        functools.partial(_proj_kernel, strip=min(256, tm)),
        grid=(T // tm, IN_WIDTH // SEG),
        in_specs=[
            pl.BlockSpec((tm, D_MODEL), lambda i, j: (i, 0)),
            pl.BlockSpec((1, D_MODEL), lambda i, j: (0, 0)),
            pl.BlockSpec((D_MODEL, SEG), lambda i, j: (0, j)),
            pl.BlockSpec((tm, RET_KEY_DIM), lambda i, j: (i % n_pos_blocks, 0)),
            pl.BlockSpec((tm, RET_KEY_DIM), lambda i, j: (i % n_pos_blocks, 0)),
            pl.BlockSpec((tm, RET_KEY_DIM), lambda i, j: (i % n_pos_blocks, 0)),
            pl.BlockSpec((1, SEG), lambda i, j: (0, 0)),
            pl.BlockSpec((1, SEG), lambda i, j: (0, 0)),
            pl.BlockSpec((256, 256), lambda i, j: (0, 0)),
        ],
        out_specs=pl.BlockSpec((tm, SEG), lambda i, j: (i, j)),
        out_shape=jax.ShapeDtypeStruct((T, IN_WIDTH), BF16),
        scratch_shapes=[pltpu.VMEM((tm, D_MODEL), BF16)],
        compiler_params=_cparams(("arbitrary", "arbitrary")),
        name="proj",
    )(x2, attn_norm_w, w_in_bf, cos_t, sine_t, sino_t, qnw_t, knw_t, gsum)


def _attn_kernel(q_ref, k_ref, v_ref, lam_ref, subw_ref, o_ref, q2_scr, s_scr, m_scr, acc_scr, *,
                 lam_init, strip):
    qi = pl.program_id(2)
    tq = q_ref.shape[0]
    tk = tq // 2
    dv = DA_V_HEAD_DIM
    n_strips = 2 * tq // strip

    q = q_ref[...]
    lane = lax.broadcasted_iota(I32, q.shape, 1)
    zero = jnp.zeros_like(q)
    q2_scr[0:tq, :] = jnp.where(lane < DA_HEAD_DIM, q, zero)
    q2_scr[tq:2 * tq, :] = jnp.where(lane >= DA_HEAD_DIM, q, zero)
    m_scr[...] = jnp.full(m_scr.shape, NEG_BIG, F32)
    acc_scr[...] = jnp.zeros(acc_scr.shape, F32)
    ones = jnp.ones((tk, dv), BF16)

    def scores(j, slot, second_half_only=False):
        off = pl.multiple_of(j * tk, tk)
        k = k_ref[pl.ds(off, tk), :]
        spans = [(tk, tq), (tq + tk, 2 * tq)] if second_half_only else [(0, 2 * tq)]
        for a, b in spans:
            s_scr[slot, a:b, :] = lax.dot_general(q2_scr[a:b, :], k, (((1,), (1,)), ((), ())),
                                                  preferred_element_type=F32)

    VISIBLE, HIDDEN = "visible", "hidden"

    def all_visible(r):
        return VISIBLE

    def first_half_causal(r):
        q0 = (r * strip) % tq
        return q0 if q0 < tk else VISIBLE

    def second_half_causal(r):
        q0 = (r * strip) % tq
        return HIDDEN if q0 < tk else q0 - tk

    def softmax_pv(j, slot, strip_view):
        off = pl.multiple_of(j * tk, tk)
        vext = jnp.concatenate([v_ref[pl.ds(off, tk), :], ones], axis=1)
        for r in range(n_strips):
            view = strip_view(r)
            if view == HIDDEN:
                continue
            rows = slice(r * strip, (r + 1) * strip)
            s = s_scr[slot, rows, :]
            if view != VISIBLE:
                qrow = view + lax.broadcasted_iota(I32, (strip, tk), 0)
                col = lax.broadcasted_iota(I32, (strip, tk), 1)
                s = jnp.where(col > qrow, NEG_BIG, s)
            m_prev = m_scr[rows, :]
            m_new = jnp.maximum(m_prev, jnp.max(s, axis=1, keepdims=True))
            alpha = jnp.exp2(m_prev - m_new)
            p = jnp.exp2(s - jnp.concatenate([m_new] * (tk // 128), axis=1))
            pv = jnp.dot(p.astype(BF16), vext, preferred_element_type=F32)
            acc_scr[rows, :] = jnp.concatenate([alpha, alpha], axis=1) * acc_scr[rows, :] + pv
            m_scr[rows, :] = m_new

    scores(0, 0)

    def body(i, carry):
        j = 2 * i
        scores(j + 1, 1)
        softmax_pv(j, 0, all_visible)
        scores(j + 2, 0)
        softmax_pv(j + 1, 1, all_visible)
        return carry

    lax.fori_loop(0, qi, body, 0)
    scores(2 * qi + 1, 1, second_half_only=True)
    softmax_pv(2 * qi, 0, first_half_causal)
    softmax_pv(2 * qi + 1, 1, second_half_causal)

    o1 = acc_scr[0:tq, 0:dv] / acc_scr[0:tq, dv:2 * dv]
    o2 = acc_scr[tq:2 * tq, 0:dv] / acc_scr[tq:2 * tq, dv:2 * dv]
    lam_p = lam_ref[...]
    lam = (jnp.exp(jnp.sum(lam_p[0:1, :] * lam_p[1:2, :], axis=1, keepdims=True))
           - jnp.exp(jnp.sum(lam_p[2:3, :] * lam_p[3:4, :], axis=1, keepdims=True)) + lam_init)
    o = o1 - lam * o2
    r = lax.rsqrt(jnp.mean(o * o, axis=-1, keepdims=True) + EPS)
    o_ref[...] = ((o * r) * subw_ref[...] * (1.0 - lam_init)).astype(o_ref.dtype)


def _attn(P, lam_p, subw, batch, seq, tq, lam_init):
    T = P.shape[0]
    nq = seq // tq
    q_col = J_DA_Q * (SEG // 128)
    k_col = J_DA_K * (SEG // 128)
    v_col = J_DA_V * (SEG // 128)
    return pl.pallas_call(
        functools.partial(_attn_kernel, lam_init=lam_init, strip=min(128, tq)),
        grid=(batch, DA_HEADS, nq),
        in_specs=[
            pl.BlockSpec((tq, 128), lambda b, h, qi: (b * nq + qi, q_col + h)),
            pl.BlockSpec((seq, 128), lambda b, h, qi: (b, k_col + h)),
            pl.BlockSpec((seq, 128), lambda b, h, qi: (b, v_col + h)),
            pl.BlockSpec((4, DA_HEAD_DIM), lambda b, h, qi: (0, 0)),
            pl.BlockSpec((1, DA_V_HEAD_DIM), lambda b, h, qi: (0, 0)),
        ],
        out_specs=pl.BlockSpec((tq, 128), lambda b, h, qi: (b * nq + qi, h)),
        out_shape=jax.ShapeDtypeStruct((T, DA_HEADS * DA_V_HEAD_DIM), BF16),
        scratch_shapes=[
            pltpu.VMEM((2 * tq, 128), BF16),
            pltpu.VMEM((2, 2 * tq, tq // 2), F32),
            pltpu.VMEM((2 * tq, 128), F32),
            pltpu.VMEM((2 * tq, 2 * DA_V_HEAD_DIM), F32),
        ],
        compiler_params=_cparams(("arbitrary", "arbitrary", "arbitrary")),
        name="attn",
    )(P, P, P, lam_p, subw)


def _ret_kernel(q_ref, k_ref, va_ref, vb_ref, ga_ref, gb_ref, dec_ref, xi_ref, zeta_ref, cd_ref, rw_ref,
                o_ref, st_scr):
    n = pl.program_id(1)
    ts = q_ref.shape[0]
    C = RET_CHUNK
    dk, dv = RET_KEY_DIM, RET_VAL_DIM
    heads_per_tile = SEG // dv

    @pl.when(n == 0)
    def _():
        st_scr[...] = jnp.zeros(st_scr.shape, F32)

    for c in range(ts // C):
        rows = slice(c * C, (c + 1) * C)
        for h in range(RET_HEADS):
            kcols = slice(h * dk, (h + 1) * dk)
            v_ref, g_ref = (va_ref, ga_ref) if h < heads_per_tile else (vb_ref, gb_ref)
            vcols = slice((h % heads_per_tile) * dv, (h % heads_per_tile + 1) * dv)
            qc = q_ref[rows, kcols]
            kc = k_ref[rows, kcols]
            vc = v_ref[rows, vcols]
            inner = lax.dot_general(qc, kc, (((1,), (1,)), ((), ())), preferred_element_type=F32) * dec_ref[h]
            st = st_scr[h]
            qx = (qc.astype(F32) * xi_ref[h]).astype(BF16)
            o = (jnp.dot(inner.astype(BF16), vc, preferred_element_type=F32)
                 + jnp.dot(qx, st.astype(BF16), preferred_element_type=F32))
            kz = (kc.astype(F32) * zeta_ref[h]).astype(BF16)
            st_scr[h] = cd_ref[h] * st + lax.dot_general(kz, vc, (((0,), (0,)), ((), ())),
                                                         preferred_element_type=F32)
            r = lax.rsqrt(jnp.mean(o * o, axis=-1, keepdims=True) + EPS)
            o_ref[rows, h * dv:(h + 1) * dv] = (
                g_ref[rows, vcols].astype(F32) * ((o * r) * rw_ref[...])).astype(o_ref.dtype)


def _ret(P, decay, xi, zeta, cd, rw, batch, seq, ts):
    T = P.shape[0]
    ns = seq // ts
    C = RET_CHUNK
    tile = lambda j: pl.BlockSpec((ts, SEG), lambda b, n: (b * ns + n, j))
    full = lambda shape: pl.BlockSpec(shape, lambda b, n: (0,) * len(shape))
    return pl.pallas_call(
        _ret_kernel,
        grid=(batch, ns),
        in_specs=[
            tile(J_RET_Q), tile(J_RET_K), tile(5), tile(6), tile(J_RET_G0), tile(J_RET_G1),
            full((RET_HEADS, C, C)),
            full((RET_HEADS, C, 1)),
            full((RET_HEADS, C, 1)),
            full((RET_HEADS, 1, 1)),
            full((1, RET_VAL_DIM)),
        ],
        out_specs=pl.BlockSpec((ts, RET_HEADS * RET_VAL_DIM), lambda b, n: (b * ns + n, 0)),
        out_shape=jax.ShapeDtypeStruct((T, RET_HEADS * RET_VAL_DIM), BF16),
        scratch_shapes=[pltpu.VMEM((RET_HEADS, RET_KEY_DIM, RET_VAL_DIM), F32)],
        compiler_params=_cparams(("arbitrary", "arbitrary")),
        name="ret",
    )(P, P, P, P, P, P, decay, xi, zeta, cd, rw)


def _merge_kernel(x_ref, oa_ref, rb_ref, ga_ref, gb_ref, wa_ref, wb_ref, wo_ref, mnw_ref,
                  wrh_ref, wrl_ref, br_ref, ltri_ref,
                  x1_ref, hm_ref, ri_ref, rw_ref, cnt_ref, carry_scr, *, strip):
    i = pl.program_id(0)
    tm = x_ref.shape[0]

    @pl.when(i == 0)
    def _():
        carry_scr[...] = jnp.zeros(carry_scr.shape, F32)

    carry = carry_scr[...]
    for s in range(tm // strip):
        rows = slice(s * strip, (s + 1) * strip)
        ya = jnp.dot(oa_ref[rows, :], wa_ref[...], preferred_element_type=F32)
        yb = jnp.dot(rb_ref[rows, :], wb_ref[...], preferred_element_type=F32)
        z = ga_ref[rows, :].astype(F32) * ya + gb_ref[rows, :].astype(F32) * yb
        x1 = x_ref[rows, :] + jnp.dot(z.astype(BF16), wo_ref[...], preferred_element_type=F32)
        x1_ref[rows, :] = x1
        r = lax.rsqrt(jnp.mean(x1 * x1, axis=-1, keepdims=True) + EPS)
        hm = (x1 * r) * mnw_ref[...]
        hm_ref[rows, :] = _pack_bf16_pair(hm[:, 0:D_MODEL // 2], hm[:, D_MODEL // 2:D_MODEL])

        hm_hi = hm.astype(BF16)
        hm_lo = (hm - hm_hi.astype(F32)).astype(BF16)
        logits = (jnp.dot(hm_hi, wrh_ref[...], preferred_element_type=F32)
                  + jnp.dot(hm_lo, wrh_ref[...], preferred_element_type=F32)
                  + jnp.dot(hm_hi, wrl_ref[...], preferred_element_type=F32)) + br_ref[...]

        lane = lax.broadcasted_iota(I32, (strip, ROUTE_LANES), 1)
        lane_f = lane.astype(F32)

        def first_argmax(vals, vmax):
            return jnp.min(jnp.where(vals == vmax, lane_f, float(ROUTE_LANES)), axis=1, keepdims=True).astype(I32)

        gmask = lane < N_GROUPS
        lg = jnp.where(gmask, logits, NEG_BIG)
        mg = jnp.max(lg, axis=1, keepdims=True)
        g_idx = first_argmax(lg, mg)
        p_g = 1.0 / jnp.sum(jnp.where(gmask, jnp.exp(lg - mg), 0.0), axis=1, keepdims=True)

        lo_lane = N_GROUPS + EXPERTS_PER_GROUP * g_idx
        emask = (lane >= lo_lane) & (lane < lo_lane + EXPERTS_PER_GROUP)
        le = jnp.where(emask, logits, NEG_BIG)
        m1 = jnp.max(le, axis=1, keepdims=True)
        i1 = first_argmax(le, m1)
        le2 = jnp.where(lane == i1, NEG_BIG, le)
        m2 = jnp.max(le2, axis=1, keepdims=True)
        i2 = first_argmax(le2, m2)
        e21 = jnp.exp(m2 - m1)
        w_top1 = p_g / (1.0 + e21)
        w_top2 = p_g * e21 / (1.0 + e21)

        a1 = i1 - lo_lane
        a2 = i2 - lo_lane
        first_is_lo = a1 < a2
        a_lo = jnp.where(first_is_lo, a1, a2)
        a_hi = jnp.where(first_is_lo, a2, a1)
        w_lo = jnp.where(first_is_lo, w_top1, w_top2)
        w_hi = jnp.where(first_is_lo, w_top2, w_top1)
        pair = ((a_lo * (2 * EXPERTS_PER_GROUP - 1 - a_lo)) >> 1) + (a_hi - a_lo - 1)
        bucket = g_idx * PAIRS_PER_GROUP + pair

        oh = jnp.where(lane == bucket, 1.0, 0.0)
        prefix = jnp.dot(ltri_ref[...], oh.astype(BF16), preferred_element_type=F32) + carry
        rank = jnp.sum(jnp.where(lane == bucket, prefix, 0.0), axis=1, keepdims=True).astype(I32)
        carry = carry + jnp.sum(oh, axis=0, keepdims=True)

        zero_i = jnp.zeros((strip, ROUTE_LANES), I32)
        ri_ref[rows, :] = jnp.where(lane == 0, (bucket << RANK_BITS) | rank, zero_i)
        rw_ref[rows, :] = jnp.where(lane == 0, w_lo, jnp.where(lane == 1, w_hi, 0.0))
    carry_scr[...] = carry
    cnt_ref[...] = carry


def _merge(x2, oa, rb, P, wa, wb, wo, mnw, wr_hi, wr_lo, br, ltri, tm):
    T = x2.shape[0]
    full = lambda shape: pl.BlockSpec(shape, lambda i: (0,) * len(shape))
    return pl.pallas_call(
        functools.partial(_merge_kernel, strip=ltri.shape[0]),
        grid=(T // tm,),
        in_specs=[
            pl.BlockSpec((tm, D_MODEL), lambda i: (i, 0)),
            pl.BlockSpec((tm, D_MODEL), lambda i: (i, 0)),
            pl.BlockSpec((tm, 2 * D_MODEL), lambda i: (i, 0)),
            pl.BlockSpec((tm, SEG), lambda i: (i, J_GATE_A)),
            pl.BlockSpec((tm, SEG), lambda i: (i, J_GATE_B)),
            full((D_MODEL, D_MODEL)),
            full((2 * D_MODEL, D_MODEL)),
            full((D_MODEL, D_MODEL)),
            full((1, D_MODEL)),
            full((D_MODEL, ROUTE_LANES)),
            full((D_MODEL, ROUTE_LANES)),
            full((1, ROUTE_LANES)),
            full(ltri.shape),
        ],
        out_specs=[
            pl.BlockSpec((tm, D_MODEL), lambda i: (i, 0)),
            pl.BlockSpec((tm, D_MODEL // 2), lambda i: (i, 0)),
            pl.BlockSpec((tm, ROUTE_LANES), lambda i: (i, 0)),
            pl.BlockSpec((tm, ROUTE_LANES), lambda i: (i, 0)),
            full((1, ROUTE_LANES)),
        ],
        out_shape=[
            jax.ShapeDtypeStruct((T, D_MODEL), F32),
            jax.ShapeDtypeStruct((T, D_MODEL // 2), U32),
            jax.ShapeDtypeStruct((T, ROUTE_LANES), I32),
            jax.ShapeDtypeStruct((T, ROUTE_LANES), F32),
            jax.ShapeDtypeStruct((1, ROUTE_LANES), F32),
        ],
        scratch_shapes=[pltpu.VMEM((1, ROUTE_LANES), F32)],
        compiler_params=_cparams(("arbitrary",)),
        name="merge",
    )(x2, oa, rb, P, P, wa, wb, wo, mnw, wr_hi, wr_lo, br, ltri)


def _sorted_row(route_ref, seg_ref, t):
    word = route_ref[t]
    return seg_ref[word >> RANK_BITS] + (word & ((1 << RANK_BITS) - 1))


def _dispatch_kernel(route_ref, seg_ref, hm_ref, xs_in_ref, xs_ref, sem):
    del xs_in_ref
    i = pl.program_id(0)
    tm = hm_ref.shape[0]
    base = i * tm

    def row_copy(r):
        d = _sorted_row(route_ref, seg_ref, base + r)
        return pltpu.make_async_copy(hm_ref.at[pl.ds(r, 1), :], xs_ref.at[pl.ds(d, 1), :], sem)

    for r in range(tm):
        row_copy(r).start()
    for r in range(tm):
        row_copy(r).wait()


def _dispatch(route, seg_start, hm, xs_zero, tm):
    T = hm.shape[0]
    return pl.pallas_call(
        _dispatch_kernel,
        grid_spec=pltpu.PrefetchScalarGridSpec(
            num_scalar_prefetch=2,
            grid=(T // tm,),
            in_specs=[
                pl.BlockSpec((tm, hm.shape[1]), lambda i, route, seg: (i, 0)),
                pl.BlockSpec(memory_space=pl.ANY),
            ],
            out_specs=pl.BlockSpec(memory_space=pl.ANY),
            scratch_shapes=[pltpu.SemaphoreType.DMA(())],
        ),
        out_shape=jax.ShapeDtypeStruct(xs_zero.shape, xs_zero.dtype),
        input_output_aliases={3: 0},
        compiler_params=_cparams(("arbitrary",)),
        name="dispatch",
    )(route, seg_start, hm, xs_zero)


def _experts_kernel(tlo_ref, thi_ref, tvalid_ref, x_ref, w1l_ref, w3l_ref, w2l_ref,
                    w1h_ref, w3h_ref, w2h_ref, o_ref):
    del tlo_ref, thi_ref
    i = pl.program_id(0)

    @pl.when(tvalid_ref[i] == 1)
    def _():
        x_lo, x_hi = _unpack_bf16_pair(x_ref[...])
        x = jnp.concatenate([x_lo, x_hi], axis=1).astype(BF16)

        def mlp(w1_ref, w3_ref, w2_ref):
            a = jnp.dot(x, w1_ref[0], preferred_element_type=F32)
            b = jnp.dot(x, w3_ref[0], preferred_element_type=F32)
            h = (a * jax.nn.sigmoid(a)) * b
            return jnp.dot(h.astype(BF16), w2_ref[0], preferred_element_type=F32)

        o_ref[...] = _pack_bf16_pair(mlp(w1l_ref, w3l_ref, w2l_ref), mlp(w1h_ref, w3h_ref, w2h_ref))

    @pl.when(tvalid_ref[i] == 0)
    def _():
        o_ref[...] = jnp.zeros(o_ref.shape, o_ref.dtype)


def _experts(tile_lo, tile_hi, tile_valid, xs, w1, w3, w2, tr):
    R = xs.shape[0]
    w_up = lambda which: pl.BlockSpec(
        (1, D_MODEL, D_FF), lambda i, tlo, thi, tv: ((tlo if which == 0 else thi)[i], 0, 0))
    w_dn = lambda which: pl.BlockSpec(
        (1, D_FF, D_MODEL), lambda i, tlo, thi, tv: ((tlo if which == 0 else thi)[i], 0, 0))
    return pl.pallas_call(
        _experts_kernel,
        grid_spec=pltpu.PrefetchScalarGridSpec(
            num_scalar_prefetch=3,
            grid=(R // tr,),
            in_specs=[
                pl.BlockSpec((tr, D_MODEL // 2), lambda i, tlo, thi, tv: (i, 0)),
                w_up(0), w_up(0), w_dn(0), w_up(1), w_up(1), w_dn(1),
            ],
            out_specs=pl.BlockSpec((tr, D_MODEL), lambda i, tlo, thi, tv: (i, 0)),
        ),
        out_shape=jax.ShapeDtypeStruct((R, D_MODEL), U32),
        compiler_params=_cparams(("arbitrary",)),
        name="experts",
    )(tile_lo, tile_hi, tile_valid, xs, w1, w3, w2, w1, w3, w2)


def _combine_kernel(route_ref, seg_ref, x1_ref, rw_ref, ys_ref, o_ref, buf, sem):
    i = pl.program_id(0)
    n_blocks = pl.num_programs(0) - 1
    tm = x1_ref.shape[0]

    def row_copy(block, slot, r):
        d = _sorted_row(route_ref, seg_ref, block * tm + r)
        return pltpu.make_async_copy(ys_ref.at[pl.ds(d, 1), :], buf.at[slot, pl.ds(r, 1), :], sem.at[slot])

    @pl.when(i < n_blocks)
    def _():
        for r in range(tm):
            row_copy(i, i % 2, r).start()

    @pl.when(i > 0)
    def _():
        slot = (i - 1) % 2
        for r in range(tm):
            row_copy(i - 1, slot, r).wait()
        rw = rw_ref[...]
        y_lo, y_hi = _unpack_bf16_pair(buf[slot])
        o_ref[...] = x1_ref[...] + rw[:, 0:1] * y_lo + rw[:, 1:2] * y_hi


def _combine(route, seg_start, x1, rw, ys, tm):
    T = x1.shape[0]
    return pl.pallas_call(
        _combine_kernel,
        grid_spec=pltpu.PrefetchScalarGridSpec(
            num_scalar_prefetch=2,
            grid=(T // tm + 1,),
            in_specs=[
                pl.BlockSpec((tm, D_MODEL), lambda i, route, seg: (jnp.maximum(i - 1, 0), 0)),
                pl.BlockSpec((tm, ROUTE_LANES), lambda i, route, seg: (jnp.maximum(i - 1, 0), 0)),
                pl.BlockSpec(memory_space=pl.ANY),
            ],
            out_specs=pl.BlockSpec((tm, D_MODEL), lambda i, route, seg: (jnp.maximum(i - 1, 0), 0)),
            scratch_shapes=[pltpu.VMEM((2, tm, D_MODEL), U32), pltpu.SemaphoreType.DMA((2,))],
        ),
        out_shape=jax.ShapeDtypeStruct((T, D_MODEL), F32),
        compiler_params=_cparams(("arbitrary",)),
        name="combine",
    )(route, seg_start, x1, rw, ys)


def _rotation_tables(seq):
    angle = 1.0 / (RET_ROT_BASE ** jnp.linspace(0.0, 1.0, RET_KEY_DIM // 2, dtype=F32))
    angle = jnp.repeat(angle, 2)
    ang = jnp.arange(seq).astype(F32)[:, None] * angle[None, :]
    sin = jnp.sin(ang)
    even = (jnp.arange(RET_KEY_DIM) % 2 == 0)[None, :]
    return jnp.cos(ang), jnp.where(even, -sin, 0.0), jnp.where(even, 0.0, sin)


def _retention_tables():
    C = RET_CHUNK
    log_g = jnp.log1p(-jnp.exp2(-5.0 - jnp.arange(RET_HEADS, dtype=F32)))
    j = jnp.arange(C, dtype=F32)
    rel = j[:, None] - j[None, :]
    decay = jnp.where(rel >= 0, jnp.exp(log_g[:, None, None] * jnp.maximum(rel, 0.0)), 0.0)
    xi = jnp.exp(log_g[:, None] * (j + 1.0))[:, :, None]
    zeta = jnp.exp(log_g[:, None] * (C - 1.0 - j))[:, :, None]
    cd = jnp.exp(log_g * C)[:, None, None]
    return decay, xi, zeta, cd


def _pair_tables():
    lo, hi = [], []
    for g in range(N_GROUPS):
        for a in range(EXPERTS_PER_GROUP):
            for b in range(a + 1, EXPERTS_PER_GROUP):
                lo.append(g * EXPERTS_PER_GROUP + a)
                hi.append(g * EXPERTS_PER_GROUP + b)
    return np.asarray(lo, np.int32), np.asarray(hi, np.int32)


def _pick(pref, n):
    t = min(pref, n)
    while n % t:
        t //= 2
    return t


def kernel(x, attn_norm_w, w_in, q_norm_w, k_norm_w, lambda_q1, lambda_k1, lambda_q2, lambda_k2,
           da_subln_w, ret_norm_w, w_branch_a, w_branch_b, w_out, moe_norm_w,
           w_group_router, b_group_router, w_expert_router, b_expert_router, w1, w3, w2):
    B, S, D = x.shape
    T = B * S
    assert T <= (1 << RANK_BITS)
    depth = attn_norm_w.shape[0]

    tm_proj = _pick(1024, S)
    tq = _pick(1024, S)
    ts_ret = _pick(512, S)
    tm_merge = _pick(512, T)
    tm_row = _pick(256, T)
    tr = 256

    cos_t, sine_t, sino_t = _rotation_tables(S)
    decay, xi, zeta, cd = _retention_tables()
    gidx = np.arange(256) // DA_HEAD_DIM
    gsum = jnp.asarray((gidx[:, None] == gidx[None, :]).astype(np.float32) / DA_HEAD_DIM, BF16)
    merge_strip = tm_merge
    ltri = jnp.asarray(np.tril(np.ones((merge_strip, merge_strip), np.float32), -1), BF16)
    pair_lo, pair_hi = _pair_tables()
    n_tiles = T // tr + N_BUCKETS
    R = n_tiles * tr

    x2 = x.reshape(T, D)
    for l in range(depth):
        lam_init = 0.8 - 0.6 * math.exp(-0.3 * l)
        P = _proj(x2, attn_norm_w[l][None, :], w_in[l].astype(BF16), cos_t, sine_t, sino_t,
                  jnp.tile(q_norm_w[l], SEG // DA_HEAD_DIM)[None, :],
                  jnp.tile(k_norm_w[l], SEG // DA_HEAD_DIM)[None, :], gsum, S, tm_proj)
        lam_p = jnp.stack([lambda_q1[l], lambda_k1[l], lambda_q2[l], lambda_k2[l]]).astype(F32)
        oa = _attn(P, lam_p, da_subln_w[l][None, :], B, S, tq, lam_init)
        rb = _ret(P, decay, xi, zeta, cd, ret_norm_w[l][None, :].astype(F32), B, S, ts_ret)

        w_router = jnp.zeros((D, ROUTE_LANES), F32)
        w_router = w_router.at[:, 0:N_GROUPS].set(w_group_router[l])
        w_router = w_router.at[:, N_GROUPS:N_GROUPS + N_EXPERTS].set(w_expert_router[l])
        wr_hi = w_router.astype(BF16)
        wr_lo = (w_router - wr_hi.astype(F32)).astype(BF16)
        b_router = jnp.zeros((1, ROUTE_LANES), F32)
        b_router = b_router.at[0, 0:N_GROUPS].set(b_group_router[l])
        b_router = b_router.at[0, N_GROUPS:N_GROUPS + N_EXPERTS].set(b_expert_router[l].reshape(-1))

        x1, hm, ri, rw, cnt = _merge(
            x2, oa, rb, P, w_branch_a[l].astype(BF16), w_branch_b[l].astype(BF16),
            w_out[l].astype(BF16), moe_norm_w[l][None, :], wr_hi, wr_lo, b_router, ltri, tm_merge)

        counts = cnt[0, 0:N_BUCKETS].astype(I32)
        tiles_per = (counts + tr - 1) // tr
        tile_end = jnp.cumsum(tiles_per)
        seg_start = (tile_end - tiles_per) * tr
        route = ri[:, 0]
        tile_ids = jnp.arange(n_tiles, dtype=I32)
        n_used = tile_end[-1]
        tile_bucket = jnp.sum((tile_end[None, :] <= jnp.minimum(tile_ids, n_used - 1)[:, None]).astype(I32), axis=1)
        tile_bucket = jnp.minimum(tile_bucket, N_BUCKETS - 1)
        tile_lo = jnp.asarray(pair_lo)[tile_bucket]
        tile_hi = jnp.asarray(pair_hi)[tile_bucket]
        tile_valid = (tile_ids < n_used).astype(I32)

        xs = _dispatch(route, seg_start, hm, jnp.zeros((R, D // 2), U32), tm_row)
        ys = _experts(tile_lo, tile_hi, tile_valid, xs, w1[l].astype(BF16), w3[l].astype(BF16),
                      w2[l].astype(BF16), tr)
        x2 = _combine(route, seg_start, x1, rw, ys, tm_row)
    return x2.reshape(B, S, D)
```

```python
import functools
import math

import numpy as np
import jax
import jax.numpy as jnp
from jax import lax
from jax.experimental import pallas as pl
from jax.experimental.pallas import tpu as pltpu

F32 = jnp.float32
BF16 = jnp.bfloat16
I32 = jnp.int32

D_MODEL = 1024
EPS = 1e-6

DA_HEADS = 8
DA_HEAD_DIM = 64
DA_V_HEAD_DIM = 128

RET_HEADS = 4
RET_KEY_DIM = 256
RET_VAL_DIM = 512
RET_CHUNK = 256
RET_ROT_BASE = 10000.0

SEG = 1024
IN_WIDTH = 11 * SEG
J_DA_Q, J_DA_K, J_DA_V, J_RET_Q, J_RET_K = 0, 1, 2, 3, 4
J_RET_G0, J_RET_G1, J_GATE_A, J_GATE_B = 7, 8, 9, 10

N_GROUPS = 4
EXPERTS_PER_GROUP = 8
N_EXPERTS = 32
D_FF = 512
PAIRS_PER_GROUP = 28
N_BUCKETS = N_GROUPS * PAIRS_PER_GROUP
ROUTE_LANES = 128
RANK_BITS = 17

NEG_BIG = -1e30
LOG2E = 1.4426950408889634

VMEM_LIMIT = 56 * 1024 * 1024


def _cparams(sem):
    return pltpu.CompilerParams(dimension_semantics=sem, vmem_limit_bytes=VMEM_LIMIT)


U32 = jnp.uint32
HI_HALF = np.uint32(0xFFFF0000)


def _pack_bf16_pair(lo, hi):
    lo_bits = lax.bitcast_convert_type(lo.astype(BF16).astype(F32), U32)
    hi_bits = lax.bitcast_convert_type(hi.astype(BF16).astype(F32), U32)
    return (lo_bits >> 16) | (hi_bits & HI_HALF)


def _unpack_bf16_pair(word):
    lo = lax.bitcast_convert_type(word << 16, F32)
    hi = lax.bitcast_convert_type(word & HI_HALF, F32)
    return lo, hi


def _proj_kernel(x_ref, nw_ref, w_ref, cos_ref, sine_ref, sino_ref, qnw_ref, knw_ref, gsum_ref, o_ref, h_scr,
                 *, strip):
    j = pl.program_id(1)
    tm = x_ref.shape[0]

    @pl.when(j == 0)
    def _():
        x = x_ref[...]
        r = lax.rsqrt(jnp.mean(x * x, axis=-1, keepdims=True) + EPS)
        h_scr[...] = ((x * r) * nw_ref[...]).astype(BF16)

    def run(epilogue):
        for s in range(tm // strip):
            rows = slice(s * strip, (s + 1) * strip)
            y = jnp.dot(h_scr[rows, :], w_ref[...], preferred_element_type=F32)
            epilogue(y, rows)

    def qk_norm(w_ref_, scale):
        def epilogue(y, rows):
            for c in range(SEG // 256):
                sl = slice(c * 256, (c + 1) * 256)
                yc = y[:, sl]
                ms = jnp.dot((yc * yc).astype(BF16), gsum_ref[...], preferred_element_type=F32)
                o_ref[rows, sl] = ((yc * lax.rsqrt(ms + EPS)) * (w_ref_[:, sl] * scale)).astype(o_ref.dtype)
        return epilogue

    def rotate(scale):
        def epilogue(y, rows):
            for c in range(SEG // 128):
                sl = slice(c * 128, (c + 1) * 128)
                tl = slice((c % 2) * 128, (c % 2 + 1) * 128)
                t = y[:, sl]
                t_next = pltpu.roll(t, 127, 1)
                t_prev = pltpu.roll(t, 1, 1)
                out = t * cos_ref[rows, tl] + t_next * sine_ref[rows, tl] + t_prev * sino_ref[rows, tl]
                o_ref[rows, sl] = (out * scale).astype(o_ref.dtype)
        return epilogue

    def pointwise(fn):
        def epilogue(y, rows):
            o_ref[rows, :] = fn(y).astype(o_ref.dtype)
        return epilogue

    @pl.when(j == J_DA_Q)
    def _():
        run(qk_norm(qnw_ref, DA_HEAD_DIM ** -0.5 * LOG2E))

    @pl.when(j == J_DA_K)
    def _():
        run(qk_norm(knw_ref, 1.0))

    @pl.when(j == J_RET_Q)
    def _():
        run(rotate(1.0))

    @pl.when(j == J_RET_K)
    def _():
        run(rotate(RET_KEY_DIM ** -0.5))

    @pl.when((j == J_DA_V) | (j == 5) | (j == 6))
    def _():
        run(pointwise(lambda y: y))

    @pl.when((j == J_RET_G0) | (j == J_RET_G1))
    def _():
        run(pointwise(lambda y: y * jax.nn.sigmoid(y)))

    @pl.when((j == J_GATE_A) | (j == J_GATE_B))
    def _():
        run(pointwise(jax.nn.sigmoid))


def _proj(x2, attn_norm_w, w_in_bf, cos_t, sine_t, sino_t, qnw_t, knw_t, gsum, seq, tm):
    T = x2.shape[0]
    n_pos_blocks = seq // tm
    return pl.pallas_call(
        functools.partial(_proj_kernel, strip=max(tm // 4, min(256, tm))),
        grid=(T // tm, IN_WIDTH // SEG),
        in_specs=[
            pl.BlockSpec((tm, D_MODEL), lambda i, j: (i, 0)),
            pl.BlockSpec((1, D_MODEL), lambda i, j: (0, 0)),
            pl.BlockSpec((D_MODEL, SEG), lambda i, j: (0, j)),
            pl.BlockSpec((tm, RET_KEY_DIM), lambda i, j: (i % n_pos_blocks, 0)),
            pl.BlockSpec((tm, RET_KEY_DIM), lambda i, j: (i % n_pos_blocks, 0)),
            pl.BlockSpec((tm, RET_KEY_DIM), lambda i, j: (i % n_pos_blocks, 0)),
            pl.BlockSpec((1, SEG), lambda i, j: (0, 0)),
            pl.BlockSpec((1, SEG), lambda i, j: (0, 0)),
            pl.BlockSpec((256, 256), lambda i, j: (0, 0)),
        ],
        out_specs=pl.BlockSpec((tm, SEG), lambda i, j: (i, j)),
        out_shape=jax.ShapeDtypeStruct((T, IN_WIDTH), BF16),
        scratch_shapes=[pltpu.VMEM((tm, D_MODEL), BF16)],
        compiler_params=_cparams(("arbitrary", "arbitrary")),
        name="proj",
    )(x2, attn_norm_w, w_in_bf, cos_t, sine_t, sino_t, qnw_t, knw_t, gsum)


def _attn_kernel(q_ref, k_ref, v_ref, lam_ref, subw_ref, o_ref, q2_scr, s_scr, m_scr, acc_scr, *,
                 lam_init, strip, n_sub):
    gi = pl.program_id(2)
    tq = q_ref.shape[0] // n_sub
    tk = tq // 2
    dv = DA_V_HEAD_DIM
    n_strips = 2 * tq // strip
    ones = jnp.ones((tk, dv), BF16)

    def prepare(u):
        q = q_ref[u * tq:(u + 1) * tq, :]
        lane = lax.broadcasted_iota(I32, q.shape, 1)
        zero = jnp.zeros_like(q)
        q2_scr[u, 0:tq, :] = jnp.where(lane < DA_HEAD_DIM, q, zero)
        q2_scr[u, tq:2 * tq, :] = jnp.where(lane >= DA_HEAD_DIM, q, zero)
        m_scr[u] = jnp.full(m_scr.shape[1:], NEG_BIG, F32)
        acc_scr[u] = jnp.zeros(acc_scr.shape[1:], F32)

    def scores(u, j, slot, second_half_only=False):
        off = pl.multiple_of(j * tk, tk)
        k = k_ref[pl.ds(off, tk), :]
        spans = [(tk, tq), (tq + tk, 2 * tq)] if second_half_only else [(0, 2 * tq)]
        for a, b in spans:
            s_scr[slot, a:b, :] = lax.dot_general(q2_scr[u, a:b, :], k, (((1,), (1,)), ((), ())),
                                                  preferred_element_type=F32)

    VISIBLE, HIDDEN = "visible", "hidden"

    def all_visible(r):
        return VISIBLE

    def first_half_causal(r):
        q0 = (r * strip) % tq
        return q0 if q0 < tk else VISIBLE

    def second_half_causal(r):
        q0 = (r * strip) % tq
        return HIDDEN if q0 < tk else q0 - tk

    def softmax_pv(u, j, slot, strip_view):
        off = pl.multiple_of(j * tk, tk)
        vext = jnp.concatenate([v_ref[pl.ds(off, tk), :], ones], axis=1)
        for r in range(n_strips):
            view = strip_view(r)
            if view == HIDDEN:
                continue
            rows = slice(r * strip, (r + 1) * strip)
            s = s_scr[slot, rows, :]
            if view != VISIBLE:
                qrow = view + lax.broadcasted_iota(I32, (strip, tk), 0)
                col = lax.broadcasted_iota(I32, (strip, tk), 1)
                s = jnp.where(col > qrow, NEG_BIG, s)
            m_prev = m_scr[u, rows, :]
            m_new = jnp.maximum(m_prev, jnp.max(s, axis=1, keepdims=True))
            alpha = jnp.exp2(m_prev - m_new)
            p = jnp.exp2(s - jnp.concatenate([m_new] * (tk // 128), axis=1))
            pv = jnp.dot(p.astype(BF16), vext, preferred_element_type=F32)
            acc_scr[u, rows, :] = jnp.concatenate([alpha, alpha], axis=1) * acc_scr[u, rows, :] + pv
            m_scr[u, rows, :] = m_new

    def finish(u):
        o1 = acc_scr[u, 0:tq, 0:dv] / acc_scr[u, 0:tq, dv:2 * dv]
        o2 = acc_scr[u, tq:2 * tq, 0:dv] / acc_scr[u, tq:2 * tq, dv:2 * dv]
        lam_p = lam_ref[...]
        lam = (jnp.exp(jnp.sum(lam_p[0:1, :] * lam_p[1:2, :], axis=1, keepdims=True))
               - jnp.exp(jnp.sum(lam_p[2:3, :] * lam_p[3:4, :], axis=1, keepdims=True)) + lam_init)
        o = o1 - lam * o2
        r = lax.rsqrt(jnp.mean(o * o, axis=-1, keepdims=True) + EPS)
        o_ref[u * tq:(u + 1) * tq, :] = ((o * r) * subw_ref[...] * (1.0 - lam_init)).astype(o_ref.dtype)

    prepare(0)
    scores(0, 0, 0)
    for u in range(n_sub):
        g = gi * n_sub + u

        def body(i, carry, u=u):
            j = 2 * i
            scores(u, j + 1, 1)
            softmax_pv(u, j, 0, all_visible)
            scores(u, j + 2, 0)
            softmax_pv(u, j + 1, 1, all_visible)
            return carry

        lax.fori_loop(0, g, body, 0)
        scores(u, 2 * g + 1, 1, second_half_only=True)
        softmax_pv(u, 2 * g, 0, first_half_causal)
        if u + 1 < n_sub:
            prepare(u + 1)
            scores(u + 1, 0, 0)
        softmax_pv(u, 2 * g + 1, 1, second_half_causal)
        finish(u)


def _attn(P, lam_p, subw, batch, seq, tq, n_sub, lam_init):
    T = P.shape[0]
    tg = tq * n_sub
    ng = seq // tg
    q_col = J_DA_Q * (SEG // 128)
    k_col = J_DA_K * (SEG // 128)
    v_col = J_DA_V * (SEG // 128)
    return pl.pallas_call(
        functools.partial(_attn_kernel, lam_init=lam_init, strip=min(128, tq), n_sub=n_sub),
        grid=(batch, DA_HEADS, ng),
        in_specs=[
            pl.BlockSpec((tg, 128), lambda b, h, gi: (b * ng + gi, q_col + h)),
            pl.BlockSpec((seq, 128), lambda b, h, gi: (b, k_col + h)),
            pl.BlockSpec((seq, 128), lambda b, h, gi: (b, v_col + h)),
            pl.BlockSpec((4, DA_HEAD_DIM), lambda b, h, gi: (0, 0)),
            pl.BlockSpec((1, DA_V_HEAD_DIM), lambda b, h, gi: (0, 0)),
        ],
        out_specs=pl.BlockSpec((tg, 128), lambda b, h, gi: (b * ng + gi, h)),
        out_shape=jax.ShapeDtypeStruct((T, DA_HEADS * DA_V_HEAD_DIM), BF16),
        scratch_shapes=[
            pltpu.VMEM((n_sub, 2 * tq, 128), BF16),
            pltpu.VMEM((2, 2 * tq, tq // 2), F32),
            pltpu.VMEM((n_sub, 2 * tq, 128), F32),
            pltpu.VMEM((n_sub, 2 * tq, 2 * DA_V_HEAD_DIM), F32),
        ],
        compiler_params=_cparams(("arbitrary", "arbitrary", "arbitrary")),
        name="attn",
    )(P, P, P, lam_p, subw)


def _ret_kernel(q_ref, k_ref, va_ref, vb_ref, ga_ref, gb_ref, dec_ref, xi_ref, zeta_ref, cd_ref, rw_ref,
                o_ref, st_scr):
    n = pl.program_id(1)
    ts = q_ref.shape[0]
    C = RET_CHUNK
    dk, dv = RET_KEY_DIM, RET_VAL_DIM
    heads_per_tile = SEG // dv

    @pl.when(n == 0)
    def _():
        st_scr[...] = jnp.zeros(st_scr.shape, F32)

    for c in range(ts // C):
        rows = slice(c * C, (c + 1) * C)
        for h in range(RET_HEADS):
            kcols = slice(h * dk, (h + 1) * dk)
            v_ref, g_ref = (va_ref, ga_ref) if h < heads_per_tile else (vb_ref, gb_ref)
            vcols = slice((h % heads_per_tile) * dv, (h % heads_per_tile + 1) * dv)
            qc = q_ref[rows, kcols]
            kc = k_ref[rows, kcols]
            vc = v_ref[rows, vcols]
            inner = lax.dot_general(qc, kc, (((1,), (1,)), ((), ())), preferred_element_type=F32) * dec_ref[h]
            st = st_scr[h]
            qx = (qc.astype(F32) * xi_ref[h]).astype(BF16)
            o = (jnp.dot(inner.astype(BF16), vc, preferred_element_type=F32)
                 + jnp.dot(qx, st.astype(BF16), preferred_element_type=F32))
            kz = (kc.astype(F32) * zeta_ref[h]).astype(BF16)
            st_scr[h] = cd_ref[h] * st + lax.dot_general(kz, vc, (((0,), (0,)), ((), ())),
                                                         preferred_element_type=F32)
            r = lax.rsqrt(jnp.mean(o * o, axis=-1, keepdims=True) + EPS)
            o_ref[rows, h * dv:(h + 1) * dv] = (
                g_ref[rows, vcols].astype(F32) * ((o * r) * rw_ref[...])).astype(o_ref.dtype)


def _ret(P, decay, xi, zeta, cd, rw, batch, seq, ts):
    T = P.shape[0]
    ns = seq // ts
    C = RET_CHUNK
    tile = lambda j: pl.BlockSpec((ts, SEG), lambda b, n: (b * ns + n, j))
    full = lambda shape: pl.BlockSpec(shape, lambda b, n: (0,) * len(shape))
    return pl.pallas_call(
        _ret_kernel,
        grid=(batch, ns),
        in_specs=[
            tile(J_RET_Q), tile(J_RET_K), tile(5), tile(6), tile(J_RET_G0), tile(J_RET_G1),
            full((RET_HEADS, C, C)),
            full((RET_HEADS, C, 1)),
            full((RET_HEADS, C, 1)),
            full((RET_HEADS, 1, 1)),
            full((1, RET_VAL_DIM)),
        ],
        out_specs=pl.BlockSpec((ts, RET_HEADS * RET_VAL_DIM), lambda b, n: (b * ns + n, 0)),
        out_shape=jax.ShapeDtypeStruct((T, RET_HEADS * RET_VAL_DIM), BF16),
        scratch_shapes=[pltpu.VMEM((RET_HEADS, RET_KEY_DIM, RET_VAL_DIM), F32)],
        compiler_params=_cparams(("arbitrary", "arbitrary")),
        name="ret",
    )(P, P, P, P, P, P, decay, xi, zeta, cd, rw)


def _merge_kernel(x_ref, oa_ref, rb_ref, ga_ref, gb_ref, wa_ref, wb_ref, wo_ref, mnw_ref,
                  wrh_ref, wrl_ref, br_ref, ltri_ref,
                  x1_ref, hm_ref, ri_ref, rw_ref, cnt_ref, carry_scr, *, strip):
    i = pl.program_id(0)
    tm = x_ref.shape[0]

    @pl.when(i == 0)
    def _():
        carry_scr[...] = jnp.zeros(carry_scr.shape, F32)

    carry = carry_scr[...]
    for s in range(tm // strip):
        rows = slice(s * strip, (s + 1) * strip)
        ya = jnp.dot(oa_ref[rows, :], wa_ref[...], preferred_element_type=F32)
        yb = jnp.dot(rb_ref[rows, :], wb_ref[...], preferred_element_type=F32)
        z = ga_ref[rows, :].astype(F32) * ya + gb_ref[rows, :].astype(F32) * yb
        x1 = x_ref[rows, :] + jnp.dot(z.astype(BF16), wo_ref[...], preferred_element_type=F32)
        x1_ref[rows, :] = x1
        r = lax.rsqrt(jnp.mean(x1 * x1, axis=-1, keepdims=True) + EPS)
        hm = (x1 * r) * mnw_ref[...]
        hm_ref[rows, :] = _pack_bf16_pair(hm[:, 0:D_MODEL // 2], hm[:, D_MODEL // 2:D_MODEL])

        hm_hi = hm.astype(BF16)
        hm_lo = (hm - hm_hi.astype(F32)).astype(BF16)
        logits = (jnp.dot(hm_hi, wrh_ref[...], preferred_element_type=F32)
                  + jnp.dot(hm_lo, wrh_ref[...], preferred_element_type=F32)
                  + jnp.dot(hm_hi, wrl_ref[...], preferred_element_type=F32)) + br_ref[...]

        lane = lax.broadcasted_iota(I32, (strip, ROUTE_LANES), 1)
        lane_f = lane.astype(F32)

        def first_argmax(vals, vmax):
            return jnp.min(jnp.where(vals == vmax, lane_f, float(ROUTE_LANES)), axis=1, keepdims=True).astype(I32)

        gmask = lane < N_GROUPS
        lg = jnp.where(gmask, logits, NEG_BIG)
        mg = jnp.max(lg, axis=1, keepdims=True)
        g_idx = first_argmax(lg, mg)
        p_g = 1.0 / jnp.sum(jnp.where(gmask, jnp.exp(lg - mg), 0.0), axis=1, keepdims=True)

        lo_lane = N_GROUPS + EXPERTS_PER_GROUP * g_idx
        emask = (lane >= lo_lane) & (lane < lo_lane + EXPERTS_PER_GROUP)
        le = jnp.where(emask, logits, NEG_BIG)
        m1 = jnp.max(le, axis=1, keepdims=True)
        i1 = first_argmax(le, m1)
        le2 = jnp.where(lane == i1, NEG_BIG, le)
        m2 = jnp.max(le2, axis=1, keepdims=True)
        i2 = first_argmax(le2, m2)
        e21 = jnp.exp(m2 - m1)
        w_top1 = p_g / (1.0 + e21)
        w_top2 = p_g * e21 / (1.0 + e21)

        a1 = i1 - lo_lane
        a2 = i2 - lo_lane
        first_is_lo = a1 < a2
        a_lo = jnp.where(first_is_lo, a1, a2)
        a_hi = jnp.where(first_is_lo, a2, a1)
        w_lo = jnp.where(first_is_lo, w_top1, w_top2)
        w_hi = jnp.where(first_is_lo, w_top2, w_top1)
        pair = ((a_lo * (2 * EXPERTS_PER_GROUP - 1 - a_lo)) >> 1) + (a_hi - a_lo - 1)
        bucket = g_idx * PAIRS_PER_GROUP + pair

        oh = jnp.where(lane == bucket, 1.0, 0.0)
        prefix = jnp.dot(ltri_ref[...], oh.astype(BF16), preferred_element_type=F32) + carry
        rank = jnp.sum(jnp.where(lane == bucket, prefix, 0.0), axis=1, keepdims=True).astype(I32)
        carry = carry + jnp.sum(oh, axis=0, keepdims=True)

        zero_i = jnp.zeros((strip, ROUTE_LANES), I32)
        ri_ref[rows, :] = jnp.where(lane == 0, (bucket << RANK_BITS) | rank, zero_i)
        rw_ref[rows, :] = jnp.where(lane == 0, w_lo, jnp.where(lane == 1, w_hi, 0.0))
    carry_scr[...] = carry
    cnt_ref[...] = carry


def _merge(x2, oa, rb, P, wa, wb, wo, mnw, wr_hi, wr_lo, br, ltri, tm):
    T = x2.shape[0]
    full = lambda shape: pl.BlockSpec(shape, lambda i: (0,) * len(shape))
    return pl.pallas_call(
        functools.partial(_merge_kernel, strip=ltri.shape[0]),
        grid=(T // tm,),
        in_specs=[
            pl.BlockSpec((tm, D_MODEL), lambda i: (i, 0)),
            pl.BlockSpec((tm, D_MODEL), lambda i: (i, 0)),
            pl.BlockSpec((tm, 2 * D_MODEL), lambda i: (i, 0)),
            pl.BlockSpec((tm, SEG), lambda i: (i, J_GATE_A)),
            pl.BlockSpec((tm, SEG), lambda i: (i, J_GATE_B)),
            full((D_MODEL, D_MODEL)),
            full((2 * D_MODEL, D_MODEL)),
            full((D_MODEL, D_MODEL)),
            full((1, D_MODEL)),
            full((D_MODEL, ROUTE_LANES)),
            full((D_MODEL, ROUTE_LANES)),
            full((1, ROUTE_LANES)),
            full(ltri.shape),
        ],
        out_specs=[
            pl.BlockSpec((tm, D_MODEL), lambda i: (i, 0)),
            pl.BlockSpec((tm, D_MODEL // 2), lambda i: (i, 0)),
            pl.BlockSpec((tm, ROUTE_LANES), lambda i: (i, 0)),
            pl.BlockSpec((tm, ROUTE_LANES), lambda i: (i, 0)),
            full((1, ROUTE_LANES)),
        ],
        out_shape=[
            jax.ShapeDtypeStruct((T, D_MODEL), F32),
            jax.ShapeDtypeStruct((T, D_MODEL // 2), U32),
            jax.ShapeDtypeStruct((T, ROUTE_LANES), I32),
            jax.ShapeDtypeStruct((T, ROUTE_LANES), F32),
            jax.ShapeDtypeStruct((1, ROUTE_LANES), F32),
        ],
        scratch_shapes=[pltpu.VMEM((1, ROUTE_LANES), F32)],
        compiler_params=_cparams(("arbitrary",)),
        name="merge",
    )(x2, oa, rb, P, P, wa, wb, wo, mnw, wr_hi, wr_lo, br, ltri)


def _sorted_row(route_ref, seg_ref, t):
    word = route_ref[t]
    return seg_ref[word >> RANK_BITS] + (word & ((1 << RANK_BITS) - 1))


def _dispatch_kernel(route_ref, seg_ref, hm_ref, xs_in_ref, xs_ref, sem):
    del xs_in_ref
    i = pl.program_id(0)
    tm = hm_ref.shape[0]
    base = i * tm

    def row_copy(r):
        d = _sorted_row(route_ref, seg_ref, base + r)
        return pltpu.make_async_copy(hm_ref.at[pl.ds(r, 1), :], xs_ref.at[pl.ds(d, 1), :], sem)

    for r in range(tm):
        row_copy(r).start()
    for r in range(tm):
        row_copy(r).wait()


def _dispatch(route, seg_start, hm, xs_zero, tm):
    T = hm.shape[0]
    return pl.pallas_call(
        _dispatch_kernel,
        grid_spec=pltpu.PrefetchScalarGridSpec(
            num_scalar_prefetch=2,
            grid=(T // tm,),
            in_specs=[
                pl.BlockSpec((tm, hm.shape[1]), lambda i, route, seg: (i, 0)),
                pl.BlockSpec(memory_space=pl.ANY),
            ],
            out_specs=pl.BlockSpec(memory_space=pl.ANY),
            scratch_shapes=[pltpu.SemaphoreType.DMA(())],
        ),
        out_shape=jax.ShapeDtypeStruct(xs_zero.shape, xs_zero.dtype),
        input_output_aliases={3: 0},
        compiler_params=_cparams(("arbitrary",)),
        name="dispatch",
    )(route, seg_start, hm, xs_zero)


def _experts_kernel(tlo_ref, thi_ref, tvalid_ref, x_ref, w1l_ref, w3l_ref, w2l_ref,
                    w1h_ref, w3h_ref, w2h_ref, o_ref):
    del tlo_ref, thi_ref
    i = pl.program_id(0)

    @pl.when(tvalid_ref[i] == 1)
    def _():
        x_lo, x_hi = _unpack_bf16_pair(x_ref[...])
        x = jnp.concatenate([x_lo, x_hi], axis=1).astype(BF16)

        def mlp(w1_ref, w3_ref, w2_ref):
            a = jnp.dot(x, w1_ref[0], preferred_element_type=F32)
            b = jnp.dot(x, w3_ref[0], preferred_element_type=F32)
            h = (a * jax.nn.sigmoid(a)) * b
            return jnp.dot(h.astype(BF16), w2_ref[0], preferred_element_type=F32)

        o_ref[...] = _pack_bf16_pair(mlp(w1l_ref, w3l_ref, w2l_ref), mlp(w1h_ref, w3h_ref, w2h_ref))

    @pl.when(tvalid_ref[i] == 0)
    def _():
        o_ref[...] = jnp.zeros(o_ref.shape, o_ref.dtype)


def _experts(tile_lo, tile_hi, tile_valid, xs, w1, w3, w2, tr):
    R = xs.shape[0]
    w_up = lambda which: pl.BlockSpec(
        (1, D_MODEL, D_FF), lambda i, tlo, thi, tv: ((tlo if which == 0 else thi)[i], 0, 0))
    w_dn = lambda which: pl.BlockSpec(
        (1, D_FF, D_MODEL), lambda i, tlo, thi, tv: ((tlo if which == 0 else thi)[i], 0, 0))
    return pl.pallas_call(
        _experts_kernel,
        grid_spec=pltpu.PrefetchScalarGridSpec(
            num_scalar_prefetch=3,
            grid=(R // tr,),
            in_specs=[
                pl.BlockSpec((tr, D_MODEL // 2), lambda i, tlo, thi, tv: (i, 0)),
                w_up(0), w_up(0), w_dn(0), w_up(1), w_up(1), w_dn(1),
            ],
            out_specs=pl.BlockSpec((tr, D_MODEL), lambda i, tlo, thi, tv: (i, 0)),
        ),
        out_shape=jax.ShapeDtypeStruct((R, D_MODEL), U32),
        compiler_params=_cparams(("arbitrary",)),
        name="experts",
    )(tile_lo, tile_hi, tile_valid, xs, w1, w3, w2, w1, w3, w2)


def _combine_kernel(route_ref, seg_ref, x1_ref, rw_ref, ys_ref, o_ref, buf, sem):
    i = pl.program_id(0)
    n_blocks = pl.num_programs(0) - 1
    tm = x1_ref.shape[0]

    def row_copy(block, slot, r):
        d = _sorted_row(route_ref, seg_ref, block * tm + r)
        return pltpu.make_async_copy(ys_ref.at[pl.ds(d, 1), :], buf.at[slot, pl.ds(r, 1), :], sem.at[slot])

    @pl.when(i < n_blocks)
    def _():
        for r in range(tm):
            row_copy(i, i % 2, r).start()

    @pl.when(i > 0)
    def _():
        slot = (i - 1) % 2
        for r in range(tm):
            row_copy(i - 1, slot, r).wait()
        rw = rw_ref[...]
        y_lo, y_hi = _unpack_bf16_pair(buf[slot])
        o_ref[...] = x1_ref[...] + rw[:, 0:1] * y_lo + rw[:, 1:2] * y_hi


def _combine(route, seg_start, x1, rw, ys, tm):
    T = x1.shape[0]
    return pl.pallas_call(
        _combine_kernel,
        grid_spec=pltpu.PrefetchScalarGridSpec(
            num_scalar_prefetch=2,
            grid=(T // tm + 1,),
            in_specs=[
                pl.BlockSpec((tm, D_MODEL), lambda i, route, seg: (jnp.maximum(i - 1, 0), 0)),
                pl.BlockSpec((tm, ROUTE_LANES), lambda i, route, seg: (jnp.maximum(i - 1, 0), 0)),
                pl.BlockSpec(memory_space=pl.ANY),
            ],
            out_specs=pl.BlockSpec((tm, D_MODEL), lambda i, route, seg: (jnp.maximum(i - 1, 0), 0)),
            scratch_shapes=[pltpu.VMEM((2, tm, D_MODEL), U32), pltpu.SemaphoreType.DMA((2,))],
        ),
        out_shape=jax.ShapeDtypeStruct((T, D_MODEL), F32),
        compiler_params=_cparams(("arbitrary",)),
        name="combine",
    )(route, seg_start, x1, rw, ys)


def _rotation_tables(seq):
    angle = 1.0 / (RET_ROT_BASE ** jnp.linspace(0.0, 1.0, RET_KEY_DIM // 2, dtype=F32))
    angle = jnp.repeat(angle, 2)
    ang = jnp.arange(seq).astype(F32)[:, None] * angle[None, :]
    sin = jnp.sin(ang)
    even = (jnp.arange(RET_KEY_DIM) % 2 == 0)[None, :]
    return jnp.cos(ang), jnp.where(even, -sin, 0.0), jnp.where(even, 0.0, sin)


def _retention_tables():
    C = RET_CHUNK
    log_g = jnp.log1p(-jnp.exp2(-5.0 - jnp.arange(RET_HEADS, dtype=F32)))
    j = jnp.arange(C, dtype=F32)
    rel = j[:, None] - j[None, :]
    decay = jnp.where(rel >= 0, jnp.exp(log_g[:, None, None] * jnp.maximum(rel, 0.0)), 0.0)
    xi = jnp.exp(log_g[:, None] * (j + 1.0))[:, :, None]
    zeta = jnp.exp(log_g[:, None] * (C - 1.0 - j))[:, :, None]
    cd = jnp.exp(log_g * C)[:, None, None]
    return decay, xi, zeta, cd


def _pair_tables():
    lo, hi = [], []
    for g in range(N_GROUPS):
        for a in range(EXPERTS_PER_GROUP):
            for b in range(a + 1, EXPERTS_PER_GROUP):
                lo.append(g * EXPERTS_PER_GROUP + a)
                hi.append(g * EXPERTS_PER_GROUP + b)
    return np.asarray(lo, np.int32), np.asarray(hi, np.int32)


def _pick(pref, n):
    t = min(pref, n)
    while n % t:
        t //= 2
    return t


def kernel(x, attn_norm_w, w_in, q_norm_w, k_norm_w, lambda_q1, lambda_k1, lambda_q2, lambda_k2,
           da_subln_w, ret_norm_w, w_branch_a, w_branch_b, w_out, moe_norm_w,
           w_group_router, b_group_router, w_expert_router, b_expert_router, w1, w3, w2):
    B, S, D = x.shape
    T = B * S
    assert T <= (1 << RANK_BITS)
    depth = attn_norm_w.shape[0]

    tm_proj = _pick(2048, S)
    tq = _pick(1024, S)
    n_sub = 2 if S % (2 * tq) == 0 else 1
    ts_ret = _pick(512, S)
    tm_merge = _pick(512, T)
    tm_row = _pick(256, T)
    tr = 256

    cos_t, sine_t, sino_t = _rotation_tables(S)
    decay, xi, zeta, cd = _retention_tables()
    gidx = np.arange(256) // DA_HEAD_DIM
    gsum = jnp.asarray((gidx[:, None] == gidx[None, :]).astype(np.float32) / DA_HEAD_DIM, BF16)
    merge_strip = tm_merge
    ltri = jnp.asarray(np.tril(np.ones((merge_strip, merge_strip), np.float32), -1), BF16)
    pair_lo, pair_hi = _pair_tables()
    n_tiles = T // tr + N_BUCKETS
    R = n_tiles * tr

    x2 = x.reshape(T, D)
    for l in range(depth):
        lam_init = 0.8 - 0.6 * math.exp(-0.3 * l)
        P = _proj(x2, attn_norm_w[l][None, :], w_in[l].astype(BF16), cos_t, sine_t, sino_t,
                  jnp.tile(q_norm_w[l], SEG // DA_HEAD_DIM)[None, :],
                  jnp.tile(k_norm_w[l], SEG // DA_HEAD_DIM)[None, :], gsum, S, tm_proj)
        lam_p = jnp.stack([lambda_q1[l], lambda_k1[l], lambda_q2[l], lambda_k2[l]]).astype(F32)
        oa = _attn(P, lam_p, da_subln_w[l][None, :], B, S, tq, n_sub, lam_init)
        rb = _ret(P, decay, xi, zeta, cd, ret_norm_w[l][None, :].astype(F32), B, S, ts_ret)

        w_router = jnp.zeros((D, ROUTE_LANES), F32)
        w_router = w_router.at[:, 0:N_GROUPS].set(w_group_router[l])
        w_router = w_router.at[:, N_GROUPS:N_GROUPS + N_EXPERTS].set(w_expert_router[l])
        wr_hi = w_router.astype(BF16)
        wr_lo = (w_router - wr_hi.astype(F32)).astype(BF16)
        b_router = jnp.zeros((1, ROUTE_LANES), F32)
        b_router = b_router.at[0, 0:N_GROUPS].set(b_group_router[l])
        b_router = b_router.at[0, N_GROUPS:N_GROUPS + N_EXPERTS].set(b_expert_router[l].reshape(-1))

        x1, hm, ri, rw, cnt = _merge(
            x2, oa, rb, P, w_branch_a[l].astype(BF16), w_branch_b[l].astype(BF16),
            w_out[l].astype(BF16), moe_norm_w[l][None, :], wr_hi, wr_lo, b_router, ltri, tm_merge)

        counts = cnt[0, 0:N_BUCKETS].astype(I32)
        tiles_per = (counts + tr - 1) // tr
        tile_end = jnp.cumsum(tiles_per)
        seg_start = (tile_end - tiles_per) * tr
        route = ri[:, 0]
        tile_ids = jnp.arange(n_tiles, dtype=I32)
        n_used = tile_end[-1]
        tile_bucket = jnp.sum((tile_end[None, :] <= jnp.minimum(tile_ids, n_used - 1)[:, None]).astype(I32), axis=1)
        tile_bucket = jnp.minimum(tile_bucket, N_BUCKETS - 1)
        tile_lo = jnp.asarray(pair_lo)[tile_bucket]
        tile_hi = jnp.asarray(pair_hi)[tile_bucket]
        tile_valid = (tile_ids < n_used).astype(I32)

        xs = _dispatch(route, seg_start, hm, jnp.zeros((R, D // 2), U32), tm_row)
        ys = _experts(tile_lo, tile_hi, tile_valid, xs, w1[l].astype(BF16), w3[l].astype(BF16),
                      w2[l].astype(BF16), tr)
        x2 = _combine(route, seg_start, x1, rw, ys, tm_row)
    return x2.reshape(B, S, D)
```

```python
import functools
import math

import numpy as np
import jax
import jax.numpy as jnp
from jax import lax
from jax.experimental import pallas as pl
from jax.experimental.pallas import tpu as pltpu

F32 = jnp.float32
BF16 = jnp.bfloat16
I32 = jnp.int32

D_MODEL = 1024
EPS = 1e-6

DA_HEADS = 8
DA_HEAD_DIM = 64
DA_V_HEAD_DIM = 128

RET_HEADS = 4
RET_KEY_DIM = 256
RET_VAL_DIM = 512
RET_CHUNK = 256
RET_ROT_BASE = 10000.0

SEG = 1024
IN_WIDTH = 11 * SEG
J_DA_Q, J_DA_K, J_DA_V, J_RET_Q, J_RET_K = 0, 1, 2, 3, 4
J_RET_G0, J_RET_G1, J_GATE_A, J_GATE_B = 7, 8, 9, 10

N_GROUPS = 4
EXPERTS_PER_GROUP = 8
N_EXPERTS = 32
D_FF = 512
PAIRS_PER_GROUP = 28
N_BUCKETS = N_GROUPS * PAIRS_PER_GROUP
ROUTE_LANES = 128
RANK_BITS = 17

NEG_BIG = -1e30
LOG2E = 1.4426950408889634

VMEM_LIMIT = 56 * 1024 * 1024


def _cparams(sem):
    return pltpu.CompilerParams(dimension_semantics=sem, vmem_limit_bytes=VMEM_LIMIT)


U32 = jnp.uint32
HI_HALF = np.uint32(0xFFFF0000)


def _pack_bf16_pair(lo, hi):
    lo_bits = lax.bitcast_convert_type(lo.astype(BF16).astype(F32), U32)
    hi_bits = lax.bitcast_convert_type(hi.astype(BF16).astype(F32), U32)
    return (lo_bits >> 16) | (hi_bits & HI_HALF)


def _unpack_bf16_pair(word):
    lo = lax.bitcast_convert_type(word << 16, F32)
    hi = lax.bitcast_convert_type(word & HI_HALF, F32)
    return lo, hi


def _proj_kernel(x_ref, nw_ref, w_ref, cos_ref, sine_ref, sino_ref, qnw_ref, knw_ref, gsum_ref, o_ref, h_scr,
                 *, strip):
    j = pl.program_id(1)
    tm = x_ref.shape[0]

    @pl.when(j == 0)
    def _():
        x = x_ref[...]
        r = lax.rsqrt(jnp.mean(x * x, axis=-1, keepdims=True) + EPS)
        h_scr[...] = ((x * r) * nw_ref[...]).astype(BF16)

    def run(epilogue):
        for s in range(tm // strip):
            rows = slice(s * strip, (s + 1) * strip)
            y = jnp.dot(h_scr[rows, :], w_ref[...], preferred_element_type=F32)
            epilogue(y, rows)

    def qk_norm(w_ref_, scale):
        def epilogue(y, rows):
            for c in range(SEG // 256):
                sl = slice(c * 256, (c + 1) * 256)
                yc = y[:, sl]
                ms = jnp.dot((yc * yc).astype(BF16), gsum_ref[...], preferred_element_type=F32)
                o_ref[rows, sl] = ((yc * lax.rsqrt(ms + EPS)) * (w_ref_[:, sl] * scale)).astype(o_ref.dtype)
        return epilogue

    def rotate(scale):
        def epilogue(y, rows):
            for c in range(SEG // 128):
                sl = slice(c * 128, (c + 1) * 128)
                tl = slice((c % 2) * 128, (c % 2 + 1) * 128)
                t = y[:, sl]
                t_next = pltpu.roll(t, 127, 1)
                t_prev = pltpu.roll(t, 1, 1)
                out = t * cos_ref[rows, tl] + t_next * sine_ref[rows, tl] + t_prev * sino_ref[rows, tl]
                o_ref[rows, sl] = (out * scale).astype(o_ref.dtype)
        return epilogue

    def pointwise(fn):
        def epilogue(y, rows):
            o_ref[rows, :] = fn(y).astype(o_ref.dtype)
        return epilogue

    @pl.when(j == J_DA_Q)
    def _():
        run(qk_norm(qnw_ref, DA_HEAD_DIM ** -0.5 * LOG2E))

    @pl.when(j == J_DA_K)
    def _():
        run(qk_norm(knw_ref, 1.0))

    @pl.when(j == J_RET_Q)
    def _():
        run(rotate(1.0))

    @pl.when(j == J_RET_K)
    def _():
        run(rotate(RET_KEY_DIM ** -0.5))

    @pl.when((j == J_DA_V) | (j == 5) | (j == 6))
    def _():
        run(pointwise(lambda y: y))

    @pl.when((j == J_RET_G0) | (j == J_RET_G1))
    def _():
        run(pointwise(lambda y: y * jax.nn.sigmoid(y)))

    @pl.when((j == J_GATE_A) | (j == J_GATE_B))
    def _():
        run(pointwise(jax.nn.sigmoid))


def _proj(x2, attn_norm_w, w_in_bf, cos_t, sine_t, sino_t, qnw_t, knw_t, gsum, seq, tm):
    T = x2.shape[0]
    n_pos_blocks = seq // tm
    return pl.pallas_call(
        functools.partial(_proj_kernel, strip=max(tm // 4, min(256, tm))),
        grid=(T // tm, IN_WIDTH // SEG),
        in_specs=[
            pl.BlockSpec((tm, D_MODEL), lambda i, j: (i, 0)),
            pl.BlockSpec((1, D_MODEL), lambda i, j: (0, 0)),
            pl.BlockSpec((D_MODEL, SEG), lambda i, j: (0, j)),
            pl.BlockSpec((tm, RET_KEY_DIM), lambda i, j: (i % n_pos_blocks, 0)),
            pl.BlockSpec((tm, RET_KEY_DIM), lambda i, j: (i % n_pos_blocks, 0)),
            pl.BlockSpec((tm, RET_KEY_DIM), lambda i, j: (i % n_pos_blocks, 0)),
            pl.BlockSpec((1, SEG), lambda i, j: (0, 0)),
            pl.BlockSpec((1, SEG), lambda i, j: (0, 0)),
            pl.BlockSpec((256, 256), lambda i, j: (0, 0)),
        ],
        out_specs=pl.BlockSpec((tm, SEG), lambda i, j: (i, j)),
        out_shape=jax.ShapeDtypeStruct((T, IN_WIDTH), BF16),
        scratch_shapes=[pltpu.VMEM((tm, D_MODEL), BF16)],
        compiler_params=_cparams(("arbitrary", "arbitrary")),
        name="proj",
    )(x2, attn_norm_w, w_in_bf, cos_t, sine_t, sino_t, qnw_t, knw_t, gsum)


def _attn_kernel(q_ref, k_ref, v_ref, lam_ref, subw_ref, o_ref, q2_scr, s_scr, m_scr, acc_scr, *,
                 lam_init, strip, n_sub, tk):
    gi = pl.program_id(2)
    tq = q_ref.shape[0] // n_sub
    nk = tq // tk
    dv = DA_V_HEAD_DIM
    n_strips = 2 * tq // strip
    ones = jnp.ones((tk, dv), BF16)

    def prepare(u):
        q = q_ref[u * tq:(u + 1) * tq, :]
        lane = lax.broadcasted_iota(I32, q.shape, 1)
        zero = jnp.zeros_like(q)
        q2_scr[u, 0:tq, :] = jnp.where(lane < DA_HEAD_DIM, q, zero)
        q2_scr[u, tq:2 * tq, :] = jnp.where(lane >= DA_HEAD_DIM, q, zero)
        m_scr[u] = jnp.full(m_scr.shape[1:], NEG_BIG, F32)
        acc_scr[u] = jnp.zeros(acc_scr.shape[1:], F32)

    def scores(u, j, slot, first_row=0):
        off = pl.multiple_of(j * tk, tk)
        k = k_ref[pl.ds(off, tk), :]
        spans = [(first_row, tq), (tq + first_row, 2 * tq)] if first_row else [(0, 2 * tq)]
        for a, b in spans:
            s_scr[slot, a:b, :] = lax.dot_general(q2_scr[u, a:b, :], k, (((1,), (1,)), ((), ())),
                                                  preferred_element_type=F32)

    VISIBLE, HIDDEN = "visible", "hidden"

    def all_visible(r):
        return VISIBLE

    def diagonal_view(t):
        def view(r):
            q0 = (r * strip) % tq - t * tk
            return HIDDEN if q0 < 0 else (q0 if q0 < tk else VISIBLE)
        return view

    def softmax_pv(u, j, slot, strip_view):
        off = pl.multiple_of(j * tk, tk)
        vext = jnp.concatenate([v_ref[pl.ds(off, tk), :], ones], axis=1)
        for r in range(n_strips):
            view = strip_view(r)
            if view == HIDDEN:
                continue
            rows = slice(r * strip, (r + 1) * strip)
            s = s_scr[slot, rows, :]
            if view != VISIBLE:
                qrow = view + lax.broadcasted_iota(I32, (strip, tk), 0)
                col = lax.broadcasted_iota(I32, (strip, tk), 1)
                s = jnp.where(col > qrow, NEG_BIG, s)
            m_prev = m_scr[u, rows, :]
            m_new = jnp.maximum(m_prev, jnp.max(s, axis=1, keepdims=True))
            alpha = jnp.exp2(m_prev - m_new)
            p = jnp.exp2(s - jnp.concatenate([m_new] * (tk // 128), axis=1))
            pv = jnp.dot(p.astype(BF16), vext, preferred_element_type=F32)
            acc_scr[u, rows, :] = jnp.concatenate([alpha, alpha], axis=1) * acc_scr[u, rows, :] + pv
            m_scr[u, rows, :] = m_new

    def finish(u):
        o1 = acc_scr[u, 0:tq, 0:dv] / acc_scr[u, 0:tq, dv:2 * dv]
        o2 = acc_scr[u, tq:2 * tq, 0:dv] / acc_scr[u, tq:2 * tq, dv:2 * dv]
        lam_p = lam_ref[...]
        lam = (jnp.exp(jnp.sum(lam_p[0:1, :] * lam_p[1:2, :], axis=1, keepdims=True))
               - jnp.exp(jnp.sum(lam_p[2:3, :] * lam_p[3:4, :], axis=1, keepdims=True)) + lam_init)
        o = o1 - lam * o2
        r = lax.rsqrt(jnp.mean(o * o, axis=-1, keepdims=True) + EPS)
        o_ref[u * tq:(u + 1) * tq, :] = ((o * r) * subw_ref[...] * (1.0 - lam_init)).astype(o_ref.dtype)

    prepare(0)
    scores(0, 0, 0)
    for u in range(n_sub):
        g = gi * n_sub + u

        def body(i, carry, u=u):
            j = 2 * i
            scores(u, j + 1, 1)
            softmax_pv(u, j, 0, all_visible)
            scores(u, j + 2, 0)
            softmax_pv(u, j + 1, 1, all_visible)
            return carry

        lax.fori_loop(0, (nk // 2) * g, body, 0)
        for t in range(nk):
            if t + 1 < nk:
                scores(u, nk * g + t + 1, (t + 1) % 2, first_row=(t + 1) * tk)
            elif u + 1 < n_sub:
                prepare(u + 1)
                scores(u + 1, 0, 0)
            softmax_pv(u, nk * g + t, t % 2, diagonal_view(t))
        finish(u)


def _attn(P, lam_p, subw, batch, seq, tq, tk, n_sub, lam_init):
    T = P.shape[0]
    assert tq % (2 * tk) == 0
    tg = tq * n_sub
    ng = seq // tg
    q_col = J_DA_Q * (SEG // 128)
    k_col = J_DA_K * (SEG // 128)
    v_col = J_DA_V * (SEG // 128)
    return pl.pallas_call(
        functools.partial(_attn_kernel, lam_init=lam_init, strip=min(128, tq), n_sub=n_sub, tk=tk),
        grid=(batch, DA_HEADS, ng),
        in_specs=[
            pl.BlockSpec((tg, 128), lambda b, h, gi: (b * ng + gi, q_col + h)),
            pl.BlockSpec((seq, 128), lambda b, h, gi: (b, k_col + h)),
            pl.BlockSpec((seq, 128), lambda b, h, gi: (b, v_col + h)),
            pl.BlockSpec((4, DA_HEAD_DIM), lambda b, h, gi: (0, 0)),
            pl.BlockSpec((1, DA_V_HEAD_DIM), lambda b, h, gi: (0, 0)),
        ],
        out_specs=pl.BlockSpec((tg, 128), lambda b, h, gi: (b * ng + gi, h)),
        out_shape=jax.ShapeDtypeStruct((T, DA_HEADS * DA_V_HEAD_DIM), BF16),
        scratch_shapes=[
            pltpu.VMEM((n_sub, 2 * tq, 128), BF16),
            pltpu.VMEM((2, 2 * tq, tk), F32),
            pltpu.VMEM((n_sub, 2 * tq, 128), F32),
            pltpu.VMEM((n_sub, 2 * tq, 2 * DA_V_HEAD_DIM), F32),
        ],
        compiler_params=_cparams(("arbitrary", "arbitrary", "arbitrary")),
        name="attn",
    )(P, P, P, lam_p, subw)


def _ret_kernel(q_ref, k_ref, va_ref, vb_ref, ga_ref, gb_ref, dec_ref, xi_ref, zeta_ref, cd_ref, rw_ref,
                o_ref, st_scr):
    n = pl.program_id(1)
    ts = q_ref.shape[0]
    C = RET_CHUNK
    dk, dv = RET_KEY_DIM, RET_VAL_DIM
    heads_per_tile = SEG // dv

    @pl.when(n == 0)
    def _():
        st_scr[...] = jnp.zeros(st_scr.shape, F32)

    for c in range(ts // C):
        rows = slice(c * C, (c + 1) * C)
        for h in range(RET_HEADS):
            kcols = slice(h * dk, (h + 1) * dk)
            v_ref, g_ref = (va_ref, ga_ref) if h < heads_per_tile else (vb_ref, gb_ref)
            vcols = slice((h % heads_per_tile) * dv, (h % heads_per_tile + 1) * dv)
            qc = q_ref[rows, kcols]
            kc = k_ref[rows, kcols]
            vc = v_ref[rows, vcols]
            inner = lax.dot_general(qc, kc, (((1,), (1,)), ((), ())), preferred_element_type=F32) * dec_ref[h]
            st = st_scr[h]
            qx = (qc.astype(F32) * xi_ref[h]).astype(BF16)
            o = (jnp.dot(inner.astype(BF16), vc, preferred_element_type=F32)
                 + jnp.dot(qx, st.astype(BF16), preferred_element_type=F32))
            kz = (kc.astype(F32) * zeta_ref[h]).astype(BF16)
            st_scr[h] = cd_ref[h] * st + lax.dot_general(kz, vc, (((0,), (0,)), ((), ())),
                                                         preferred_element_type=F32)
            r = lax.rsqrt(jnp.mean(o * o, axis=-1, keepdims=True) + EPS)
            o_ref[rows, h * dv:(h + 1) * dv] = (
                g_ref[rows, vcols].astype(F32) * ((o * r) * rw_ref[...])).astype(o_ref.dtype)


def _ret(P, decay, xi, zeta, cd, rw, batch, seq, ts):
    T = P.shape[0]
    ns = seq // ts
    C = RET_CHUNK
    tile = lambda j: pl.BlockSpec((ts, SEG), lambda b, n: (b * ns + n, j))
    full = lambda shape: pl.BlockSpec(shape, lambda b, n: (0,) * len(shape))
    return pl.pallas_call(
        _ret_kernel,
        grid=(batch, ns),
        in_specs=[
            tile(J_RET_Q), tile(J_RET_K), tile(5), tile(6), tile(J_RET_G0), tile(J_RET_G1),
            full((RET_HEADS, C, C)),
            full((RET_HEADS, C, 1)),
            full((RET_HEADS, C, 1)),
            full((RET_HEADS, 1, 1)),
            full((1, RET_VAL_DIM)),
        ],
        out_specs=pl.BlockSpec((ts, RET_HEADS * RET_VAL_DIM), lambda b, n: (b * ns + n, 0)),
        out_shape=jax.ShapeDtypeStruct((T, RET_HEADS * RET_VAL_DIM), BF16),
        scratch_shapes=[pltpu.VMEM((RET_HEADS, RET_KEY_DIM, RET_VAL_DIM), F32)],
        compiler_params=_cparams(("arbitrary", "arbitrary")),
        name="ret",
    )(P, P, P, P, P, P, decay, xi, zeta, cd, rw)


def _merge_kernel(x_ref, oa_ref, rb_ref, ga_ref, gb_ref, wa_ref, wb_ref, wo_ref, mnw_ref,
                  wrh_ref, wrl_ref, br_ref, ltri_ref,
                  x1_ref, hm_ref, ri_ref, rw_ref, cnt_ref, carry_scr):
    i = pl.program_id(0)
    tm = x_ref.shape[0]

    @pl.when(i == 0)
    def _():
        carry_scr[...] = jnp.zeros(carry_scr.shape, F32)

    ya = jnp.dot(oa_ref[...], wa_ref[...], preferred_element_type=F32)
    yb = jnp.dot(rb_ref[...], wb_ref[...], preferred_element_type=F32)
    z = ga_ref[...].astype(F32) * ya + gb_ref[...].astype(F32) * yb
    x1 = x_ref[...] + jnp.dot(z.astype(BF16), wo_ref[...], preferred_element_type=F32)
    x1_ref[...] = x1
    r = lax.rsqrt(jnp.mean(x1 * x1, axis=-1, keepdims=True) + EPS)
    hm = (x1 * r) * mnw_ref[...]
    hm_ref[...] = _pack_bf16_pair(hm[:, 0:D_MODEL // 2], hm[:, D_MODEL // 2:D_MODEL])

    hm_hi = hm.astype(BF16)
    hm_lo = (hm - hm_hi.astype(F32)).astype(BF16)
    logits = (jnp.dot(hm_hi, wrh_ref[...], preferred_element_type=F32)
              + jnp.dot(hm_lo, wrh_ref[...], preferred_element_type=F32)
              + jnp.dot(hm_hi, wrl_ref[...], preferred_element_type=F32)) + br_ref[...]

    lane = lax.broadcasted_iota(I32, (tm, ROUTE_LANES), 1)
    lane_f = lane.astype(F32)

    def first_argmax(vals, vmax):
        return jnp.min(jnp.where(vals == vmax, lane_f, float(ROUTE_LANES)), axis=1, keepdims=True).astype(I32)

    gmask = lane < N_GROUPS
    lg = jnp.where(gmask, logits, NEG_BIG)
    mg = jnp.max(lg, axis=1, keepdims=True)
    g_idx = first_argmax(lg, mg)
    p_g = 1.0 / jnp.sum(jnp.where(gmask, jnp.exp(lg - mg), 0.0), axis=1, keepdims=True)

    lo_lane = N_GROUPS + EXPERTS_PER_GROUP * g_idx
    emask = (lane >= lo_lane) & (lane < lo_lane + EXPERTS_PER_GROUP)
    le = jnp.where(emask, logits, NEG_BIG)
    m1 = jnp.max(le, axis=1, keepdims=True)
    i1 = first_argmax(le, m1)
    le2 = jnp.where(lane == i1, NEG_BIG, le)
    m2 = jnp.max(le2, axis=1, keepdims=True)
    i2 = first_argmax(le2, m2)
    e21 = jnp.exp(m2 - m1)
    w_top1 = p_g / (1.0 + e21)
    w_top2 = p_g * e21 / (1.0 + e21)

    a1 = i1 - lo_lane
    a2 = i2 - lo_lane
    first_is_lo = a1 < a2
    a_lo = jnp.where(first_is_lo, a1, a2)
    a_hi = jnp.where(first_is_lo, a2, a1)
    w_lo = jnp.where(first_is_lo, w_top1, w_top2)
    w_hi = jnp.where(first_is_lo, w_top2, w_top1)
    pair = ((a_lo * (2 * EXPERTS_PER_GROUP - 1 - a_lo)) >> 1) + (a_hi - a_lo - 1)
    bucket = g_idx * PAIRS_PER_GROUP + pair

    carry = carry_scr[...]
    oh = jnp.where(lane == bucket, 1.0, 0.0)
    prefix = jnp.dot(ltri_ref[...], oh.astype(BF16), preferred_element_type=F32) + carry
    rank = jnp.sum(jnp.where(lane == bucket, prefix, 0.0), axis=1, keepdims=True).astype(I32)
    carry = carry + jnp.sum(oh, axis=0, keepdims=True)
    carry_scr[...] = carry
    cnt_ref[...] = carry

    zero_i = jnp.zeros((tm, ROUTE_LANES), I32)
    ri_ref[...] = jnp.where(lane == 0, (bucket << RANK_BITS) | rank, zero_i)
    rw_ref[...] = jnp.where(lane == 0, w_lo, jnp.where(lane == 1, w_hi, 0.0))


def _merge(x2, oa, rb, P, wa, wb, wo, mnw, wr_hi, wr_lo, br, ltri, tm):
    T = x2.shape[0]
    full = lambda shape: pl.BlockSpec(shape, lambda i: (0,) * len(shape))
    return pl.pallas_call(
        _merge_kernel,
        grid=(T // tm,),
        in_specs=[
            pl.BlockSpec((tm, D_MODEL), lambda i: (i, 0)),
            pl.BlockSpec((tm, D_MODEL), lambda i: (i, 0)),
            pl.BlockSpec((tm, 2 * D_MODEL), lambda i: (i, 0)),
            pl.BlockSpec((tm, SEG), lambda i: (i, J_GATE_A)),
            pl.BlockSpec((tm, SEG), lambda i: (i, J_GATE_B)),
            full((D_MODEL, D_MODEL)),
            full((2 * D_MODEL, D_MODEL)),
            full((D_MODEL, D_MODEL)),
            full((1, D_MODEL)),
            full((D_MODEL, ROUTE_LANES)),
            full((D_MODEL, ROUTE_LANES)),
            full((1, ROUTE_LANES)),
            full(ltri.shape),
        ],
        out_specs=[
            pl.BlockSpec((tm, D_MODEL), lambda i: (i, 0)),
            pl.BlockSpec((tm, D_MODEL // 2), lambda i: (i, 0)),
            pl.BlockSpec((tm, ROUTE_LANES), lambda i: (i, 0)),
            pl.BlockSpec((tm, ROUTE_LANES), lambda i: (i, 0)),
            full((1, ROUTE_LANES)),
        ],
        out_shape=[
            jax.ShapeDtypeStruct((T, D_MODEL), F32),
            jax.ShapeDtypeStruct((T, D_MODEL // 2), U32),
            jax.ShapeDtypeStruct((T, ROUTE_LANES), I32),
            jax.ShapeDtypeStruct((T, ROUTE_LANES), F32),
            jax.ShapeDtypeStruct((1, ROUTE_LANES), F32),
        ],
        scratch_shapes=[pltpu.VMEM((1, ROUTE_LANES), F32)],
        compiler_params=_cparams(("arbitrary",)),
        name="merge",
    )(x2, oa, rb, P, P, wa, wb, wo, mnw, wr_hi, wr_lo, br, ltri)


def _sorted_row(route_ref, seg_ref, t):
    word = route_ref[t]
    return seg_ref[word >> RANK_BITS] + (word & ((1 << RANK_BITS) - 1))


def _dispatch_kernel(route_ref, seg_ref, hm_ref, xs_in_ref, xs_ref, sem):
    del xs_in_ref
    i = pl.program_id(0)
    tm = hm_ref.shape[0]
    base = i * tm

    def row_copy(r):
        d = _sorted_row(route_ref, seg_ref, base + r)
        return pltpu.make_async_copy(hm_ref.at[pl.ds(r, 1), :], xs_ref.at[pl.ds(d, 1), :], sem)

    for r in range(tm):
        row_copy(r).start()
    for r in range(tm):
        row_copy(r).wait()


def _dispatch(route, seg_start, hm, xs_zero, tm):
    T = hm.shape[0]
    return pl.pallas_call(
        _dispatch_kernel,
        grid_spec=pltpu.PrefetchScalarGridSpec(
            num_scalar_prefetch=2,
            grid=(T // tm,),
            in_specs=[
                pl.BlockSpec((tm, hm.shape[1]), lambda i, route, seg: (i, 0)),
                pl.BlockSpec(memory_space=pl.ANY),
            ],
            out_specs=pl.BlockSpec(memory_space=pl.ANY),
            scratch_shapes=[pltpu.SemaphoreType.DMA(())],
        ),
        out_shape=jax.ShapeDtypeStruct(xs_zero.shape, xs_zero.dtype),
        input_output_aliases={3: 0},
        compiler_params=_cparams(("arbitrary",)),
        name="dispatch",
    )(route, seg_start, hm, xs_zero)


def _experts_kernel(tlo_ref, thi_ref, tvalid_ref, x_ref, w1l_ref, w3l_ref, w2l_ref,
                    w1h_ref, w3h_ref, w2h_ref, o_ref):
    del tlo_ref, thi_ref
    i = pl.program_id(0)

    @pl.when(tvalid_ref[i] == 1)
    def _():
        x_lo, x_hi = _unpack_bf16_pair(x_ref[...])
        x = jnp.concatenate([x_lo, x_hi], axis=1).astype(BF16)

        def mlp(w1_ref, w3_ref, w2_ref):
            a = jnp.dot(x, w1_ref[0], preferred_element_type=F32)
            b = jnp.dot(x, w3_ref[0], preferred_element_type=F32)
            h = (a * jax.nn.sigmoid(a)) * b
            return jnp.dot(h.astype(BF16), w2_ref[0], preferred_element_type=F32)

        o_ref[...] = _pack_bf16_pair(mlp(w1l_ref, w3l_ref, w2l_ref), mlp(w1h_ref, w3h_ref, w2h_ref))

    @pl.when(tvalid_ref[i] == 0)
    def _():
        o_ref[...] = jnp.zeros(o_ref.shape, o_ref.dtype)


def _experts(tile_lo, tile_hi, tile_valid, xs, w1, w3, w2, tr):
    R = xs.shape[0]
    w_up = lambda which: pl.BlockSpec(
        (1, D_MODEL, D_FF), lambda i, tlo, thi, tv: ((tlo if which == 0 else thi)[i], 0, 0))
    w_dn = lambda which: pl.BlockSpec(
        (1, D_FF, D_MODEL), lambda i, tlo, thi, tv: ((tlo if which == 0 else thi)[i], 0, 0))
    return pl.pallas_call(
        _experts_kernel,
        grid_spec=pltpu.PrefetchScalarGridSpec(
            num_scalar_prefetch=3,
            grid=(R // tr,),
            in_specs=[
                pl.BlockSpec((tr, D_MODEL // 2), lambda i, tlo, thi, tv: (i, 0)),
                w_up(0), w_up(0), w_dn(0), w_up(1), w_up(1), w_dn(1),
            ],
            out_specs=pl.BlockSpec((tr, D_MODEL), lambda i, tlo, thi, tv: (i, 0)),
        ),
        out_shape=jax.ShapeDtypeStruct((R, D_MODEL), U32),
        compiler_params=_cparams(("arbitrary",)),
        name="experts",
    )(tile_lo, tile_hi, tile_valid, xs, w1, w3, w2, w1, w3, w2)


def _combine_kernel(route_ref, seg_ref, x1_ref, rw_ref, ys_ref, o_ref, buf, sem):
    i = pl.program_id(0)
    n_blocks = pl.num_programs(0) - 1
    tm = x1_ref.shape[0]

    def row_copy(block, slot, r):
        d = _sorted_row(route_ref, seg_ref, block * tm + r)
        return pltpu.make_async_copy(ys_ref.at[pl.ds(d, 1), :], buf.at[slot, pl.ds(r, 1), :], sem.at[slot])

    @pl.when(i < n_blocks)
    def _():
        for r in range(tm):
            row_copy(i, i % 2, r).start()

    @pl.when(i > 0)
    def _():
        slot = (i - 1) % 2
        for r in range(tm):
            row_copy(i - 1, slot, r).wait()
        rw = rw_ref[...]
        y_lo, y_hi = _unpack_bf16_pair(buf[slot])
        o_ref[...] = x1_ref[...] + rw[:, 0:1] * y_lo + rw[:, 1:2] * y_hi


def _combine(route, seg_start, x1, rw, ys, tm):
    T = x1.shape[0]
    return pl.pallas_call(
        _combine_kernel,
        grid_spec=pltpu.PrefetchScalarGridSpec(
            num_scalar_prefetch=2,
            grid=(T // tm + 1,),
            in_specs=[
                pl.BlockSpec((tm, D_MODEL), lambda i, route, seg: (jnp.maximum(i - 1, 0), 0)),
                pl.BlockSpec((tm, ROUTE_LANES), lambda i, route, seg: (jnp.maximum(i - 1, 0), 0)),
                pl.BlockSpec(memory_space=pl.ANY),
            ],
            out_specs=pl.BlockSpec((tm, D_MODEL), lambda i, route, seg: (jnp.maximum(i - 1, 0), 0)),
            scratch_shapes=[pltpu.VMEM((2, tm, D_MODEL), U32), pltpu.SemaphoreType.DMA((2,))],
        ),
        out_shape=jax.ShapeDtypeStruct((T, D_MODEL), F32),
        compiler_params=_cparams(("arbitrary",)),
        name="combine",
    )(route, seg_start, x1, rw, ys)


def _rotation_tables(seq):
    angle = 1.0 / (RET_ROT_BASE ** jnp.linspace(0.0, 1.0, RET_KEY_DIM // 2, dtype=F32))
    angle = jnp.repeat(angle, 2)
    ang = jnp.arange(seq).astype(F32)[:, None] * angle[None, :]
    sin = jnp.sin(ang)
    even = (jnp.arange(RET_KEY_DIM) % 2 == 0)[None, :]
    return jnp.cos(ang), jnp.where(even, -sin, 0.0), jnp.where(even, 0.0, sin)


def _retention_tables():
    C = RET_CHUNK
    log_g = jnp.log1p(-jnp.exp2(-5.0 - jnp.arange(RET_HEADS, dtype=F32)))
    j = jnp.arange(C, dtype=F32)
    rel = j[:, None] - j[None, :]
    decay = jnp.where(rel >= 0, jnp.exp(log_g[:, None, None] * jnp.maximum(rel, 0.0)), 0.0)
    xi = jnp.exp(log_g[:, None] * (j + 1.0))[:, :, None]
    zeta = jnp.exp(log_g[:, None] * (C - 1.0 - j))[:, :, None]
    cd = jnp.exp(log_g * C)[:, None, None]
    return decay, xi, zeta, cd


def _pair_tables():
    lo, hi = [], []
    for g in range(N_GROUPS):
        for a in range(EXPERTS_PER_GROUP):
            for b in range(a + 1, EXPERTS_PER_GROUP):
                lo.append(g * EXPERTS_PER_GROUP + a)
                hi.append(g * EXPERTS_PER_GROUP + b)
    return np.asarray(lo, np.int32), np.asarray(hi, np.int32)


def _pick(pref, n):
    t = min(pref, n)
    while n % t:
        t //= 2
    return t


def kernel(x, attn_norm_w, w_in, q_norm_w, k_norm_w, lambda_q1, lambda_k1, lambda_q2, lambda_k2,
           da_subln_w, ret_norm_w, w_branch_a, w_branch_b, w_out, moe_norm_w,
           w_group_router, b_group_router, w_expert_router, b_expert_router, w1, w3, w2):
    B, S, D = x.shape
    T = B * S
    assert T <= (1 << RANK_BITS)
    depth = attn_norm_w.shape[0]

    tm_proj = _pick(2048, S)
    tq = _pick(2048, S)
    tk = min(512, tq // 2)
    n_sub = 1
    ts_ret = _pick(512, S)
    tm_merge = _pick(512, T)
    tm_row = _pick(512, T)
    tr = 256

    cos_t, sine_t, sino_t = _rotation_tables(S)
    decay, xi, zeta, cd = _retention_tables()
    gidx = np.arange(256) // DA_HEAD_DIM
    gsum = jnp.asarray((gidx[:, None] == gidx[None, :]).astype(np.float32) / DA_HEAD_DIM, BF16)
    ltri = jnp.asarray(np.tril(np.ones((tm_merge, tm_merge), np.float32), -1), BF16)
    pair_lo, pair_hi = _pair_tables()
    n_tiles = T // tr + N_BUCKETS
    R = n_tiles * tr

    x2 = x.reshape(T, D)
    for l in range(depth):
        lam_init = 0.8 - 0.6 * math.exp(-0.3 * l)
        P = _proj(x2, attn_norm_w[l][None, :], w_in[l].astype(BF16), cos_t, sine_t, sino_t,
                  jnp.tile(q_norm_w[l], SEG // DA_HEAD_DIM)[None, :],
                  jnp.tile(k_norm_w[l], SEG // DA_HEAD_DIM)[None, :], gsum, S, tm_proj)
        lam_p = jnp.stack([lambda_q1[l], lambda_k1[l], lambda_q2[l], lambda_k2[l]]).astype(F32)
        oa = _attn(P, lam_p, da_subln_w[l][None, :], B, S, tq, tk, n_sub, lam_init)
        rb = _ret(P, decay, xi, zeta, cd, ret_norm_w[l][None, :].astype(F32), B, S, ts_ret)

        w_router = jnp.zeros((D, ROUTE_LANES), F32)
        w_router = w_router.at[:, 0:N_GROUPS].set(w_group_router[l])
        w_router = w_router.at[:, N_GROUPS:N_GROUPS + N_EXPERTS].set(w_expert_router[l])
        wr_hi = w_router.astype(BF16)
        wr_lo = (w_router - wr_hi.astype(F32)).astype(BF16)
        b_router = jnp.zeros((1, ROUTE_LANES), F32)
        b_router = b_router.at[0, 0:N_GROUPS].set(b_group_router[l])
        b_router = b_router.at[0, N_GROUPS:N_GROUPS + N_EXPERTS].set(b_expert_router[l].reshape(-1))

        x1, hm, ri, rw, cnt = _merge(
            x2, oa, rb, P, w_branch_a[l].astype(BF16), w_branch_b[l].astype(BF16),
            w_out[l].astype(BF16), moe_norm_w[l][None, :], wr_hi, wr_lo, b_router, ltri, tm_merge)

        counts = cnt[0, 0:N_BUCKETS].astype(I32)
        tiles_per = (counts + tr - 1) // tr
        tile_end = jnp.cumsum(tiles_per)
        seg_start = (tile_end - tiles_per) * tr
        route = ri[:, 0]
        tile_ids = jnp.arange(n_tiles, dtype=I32)
        n_used = tile_end[-1]
        tile_bucket = jnp.sum((tile_end[None, :] <= jnp.minimum(tile_ids, n_used - 1)[:, None]).astype(I32), axis=1)
        tile_bucket = jnp.minimum(tile_bucket, N_BUCKETS - 1)
        tile_lo = jnp.asarray(pair_lo)[tile_bucket]
        tile_hi = jnp.asarray(pair_hi)[tile_bucket]
        tile_valid = (tile_ids < n_used).astype(I32)

        xs = _dispatch(route, seg_start, hm, jnp.zeros((R, D // 2), U32), tm_row)
        ys = _experts(tile_lo, tile_hi, tile_valid, xs, w1[l].astype(BF16), w3[l].astype(BF16),
                      w2[l].astype(BF16), tr)
        x2 = _combine(route, seg_start, x1, rw, ys, tm_row)
    return x2.reshape(B, S, D)
```

```python
import functools
import math

import numpy as np
import jax
import jax.numpy as jnp
from jax import lax
from jax.experimental import pallas as pl
from jax.experimental.pallas import tpu as pltpu

F32 = jnp.float32
BF16 = jnp.bfloat16
I32 = jnp.int32

D_MODEL = 1024
EPS = 1e-6

DA_HEADS = 8
DA_HEAD_DIM = 64
DA_V_HEAD_DIM = 128

RET_HEADS = 4
RET_KEY_DIM = 256
RET_VAL_DIM = 512
RET_CHUNK = 256
RET_ROT_BASE = 10000.0

SEG = 1024
IN_WIDTH = 11 * SEG
J_DA_Q, J_DA_K, J_DA_V, J_RET_Q, J_RET_K = 0, 1, 2, 3, 4
J_RET_G0, J_RET_G1, J_GATE_A, J_GATE_B = 7, 8, 9, 10

N_GROUPS = 4
EXPERTS_PER_GROUP = 8
N_EXPERTS = 32
D_FF = 512
PAIRS_PER_GROUP = 28
N_BUCKETS = N_GROUPS * PAIRS_PER_GROUP
ROUTE_LANES = 128
RANK_BITS = 17

NEG_BIG = -1e30
LOG2E = 1.4426950408889634

VMEM_LIMIT = 56 * 1024 * 1024


def _cparams(sem):
    return pltpu.CompilerParams(dimension_semantics=sem, vmem_limit_bytes=VMEM_LIMIT)


U32 = jnp.uint32
HI_HALF = np.uint32(0xFFFF0000)


def _pack_bf16_pair(lo, hi):
    lo_bits = lax.bitcast_convert_type(lo.astype(BF16).astype(F32), U32)
    hi_bits = lax.bitcast_convert_type(hi.astype(BF16).astype(F32), U32)
    return (lo_bits >> 16) | (hi_bits & HI_HALF)


def _unpack_bf16_pair(word):
    lo = lax.bitcast_convert_type(word << 16, F32)
    hi = lax.bitcast_convert_type(word & HI_HALF, F32)
    return lo, hi


def _proj_kernel(x_ref, nw_ref, w_ref, cos_ref, sine_ref, sino_ref, qnw_ref, knw_ref, gsum_ref, o_ref, h_scr,
                 *, strip):
    j = pl.program_id(1)
    tm = x_ref.shape[0]

    @pl.when(j == 0)
    def _():
        x = x_ref[...]
        r = lax.rsqrt(jnp.mean(x * x, axis=-1, keepdims=True) + EPS)
        h_scr[...] = ((x * r) * nw_ref[...]).astype(BF16)

    def run(epilogue):
        for s in range(tm // strip):
            rows = slice(s * strip, (s + 1) * strip)
            y = jnp.dot(h_scr[rows, :], w_ref[...], preferred_element_type=F32)
            epilogue(y, rows)

    def qk_norm(w_ref_, scale):
        def epilogue(y, rows):
            for c in range(SEG // 256):
                sl = slice(c * 256, (c + 1) * 256)
                yc = y[:, sl]
                ms = jnp.dot((yc * yc).astype(BF16), gsum_ref[...], preferred_element_type=F32)
                o_ref[rows, sl] = ((yc * lax.rsqrt(ms + EPS)) * (w_ref_[:, sl] * scale)).astype(o_ref.dtype)
        return epilogue

    def rotate(scale):
        def epilogue(y, rows):
            for c in range(SEG // 128):
                sl = slice(c * 128, (c + 1) * 128)
                tl = slice((c % 2) * 128, (c % 2 + 1) * 128)
                t = y[:, sl]
                t_next = pltpu.roll(t, 127, 1)
                t_prev = pltpu.roll(t, 1, 1)
                out = t * cos_ref[rows, tl] + t_next * sine_ref[rows, tl] + t_prev * sino_ref[rows, tl]
                o_ref[rows, sl] = (out * scale).astype(o_ref.dtype)
        return epilogue

    def pointwise(fn):
        def epilogue(y, rows):
            o_ref[rows, :] = fn(y).astype(o_ref.dtype)
        return epilogue

    @pl.when(j == J_DA_Q)
    def _():
        run(qk_norm(qnw_ref, DA_HEAD_DIM ** -0.5 * LOG2E))

    @pl.when(j == J_DA_K)
    def _():
        run(qk_norm(knw_ref, 1.0))

    @pl.when(j == J_RET_Q)
    def _():
        run(rotate(1.0))

    @pl.when(j == J_RET_K)
    def _():
        run(rotate(RET_KEY_DIM ** -0.5))

    @pl.when((j == J_DA_V) | (j == 5) | (j == 6))
    def _():
        run(pointwise(lambda y: y))

    @pl.when((j == J_RET_G0) | (j == J_RET_G1))
    def _():
        run(pointwise(lambda y: y * jax.nn.sigmoid(y)))

    @pl.when((j == J_GATE_A) | (j == J_GATE_B))
    def _():
        run(pointwise(jax.nn.sigmoid))


def _proj(x2, attn_norm_w, w_in_bf, cos_t, sine_t, sino_t, qnw_t, knw_t, gsum, seq, tm):
    T = x2.shape[0]
    n_pos_blocks = seq // tm
    table = pl.BlockSpec((tm, RET_KEY_DIM),
                         lambda i, j: ((i + jnp.where(j > J_RET_K, 1, 0)) % n_pos_blocks, 0))
    return pl.pallas_call(
        functools.partial(_proj_kernel, strip=max(tm // 4, min(256, tm))),
        grid=(T // tm, IN_WIDTH // SEG),
        in_specs=[
            pl.BlockSpec((tm, D_MODEL), lambda i, j: (i, 0)),
            pl.BlockSpec((1, D_MODEL), lambda i, j: (0, 0)),
            pl.BlockSpec((D_MODEL, SEG), lambda i, j: (0, j)),
            table, table, table,
            pl.BlockSpec((1, SEG), lambda i, j: (0, 0)),
            pl.BlockSpec((1, SEG), lambda i, j: (0, 0)),
            pl.BlockSpec((256, 256), lambda i, j: (0, 0)),
        ],
        out_specs=pl.BlockSpec((tm, SEG), lambda i, j: (i, j)),
        out_shape=jax.ShapeDtypeStruct((T, IN_WIDTH), BF16),
        scratch_shapes=[pltpu.VMEM((tm, D_MODEL), BF16)],
        compiler_params=_cparams(("arbitrary", "arbitrary")),
        name="proj",
    )(x2, attn_norm_w, w_in_bf, cos_t, sine_t, sino_t, qnw_t, knw_t, gsum)


def _attn_kernel(q_ref, k_ref, v_ref, lam_ref, subw_ref, o_ref, q2_scr, s_scr, m_scr, acc_scr, *,
                 lam_init, strip, n_sub, tk):
    gi = pl.program_id(2)
    tq = q_ref.shape[0] // n_sub
    nk = tq // tk
    dv = DA_V_HEAD_DIM
    n_strips = 2 * tq // strip
    ones = jnp.ones((tk, dv), BF16)

    def prepare(u):
        q = q_ref[u * tq:(u + 1) * tq, :]
        lane = lax.broadcasted_iota(I32, q.shape, 1)
        zero = jnp.zeros_like(q)
        q2_scr[u, 0:tq, :] = jnp.where(lane < DA_HEAD_DIM, q, zero)
        q2_scr[u, tq:2 * tq, :] = jnp.where(lane >= DA_HEAD_DIM, q, zero)
        m_scr[u] = jnp.full(m_scr.shape[1:], NEG_BIG, F32)
        acc_scr[u] = jnp.zeros(acc_scr.shape[1:], F32)

    def scores(u, j, slot, first_row=0):
        off = pl.multiple_of(j * tk, tk)
        k = k_ref[pl.ds(off, tk), :]
        spans = [(first_row, tq), (tq + first_row, 2 * tq)] if first_row else [(0, 2 * tq)]
        for a, b in spans:
            s_scr[slot, a:b, :] = lax.dot_general(q2_scr[u, a:b, :], k, (((1,), (1,)), ((), ())),
                                                  preferred_element_type=F32)

    VISIBLE, HIDDEN = "visible", "hidden"

    def all_visible(r):
        return VISIBLE

    def diagonal_view(t):
        def view(r):
            q0 = (r * strip) % tq - t * tk
            return HIDDEN if q0 < 0 else (q0 if q0 < tk else VISIBLE)
        return view

    def softmax_pv(u, j, slot, strip_view):
        off = pl.multiple_of(j * tk, tk)
        vext = jnp.concatenate([v_ref[pl.ds(off, tk), :], ones], axis=1)
        for r in range(n_strips):
            view = strip_view(r)
            if view == HIDDEN:
                continue
            rows = slice(r * strip, (r + 1) * strip)
            if view == VISIBLE:
                ncols = tk
                s = s_scr[slot, rows, :]
            else:
                ncols = min(tk, view + strip)
                s = s_scr[slot, rows, 0:ncols]
                qrow = view + lax.broadcasted_iota(I32, (strip, ncols), 0)
                col = lax.broadcasted_iota(I32, (strip, ncols), 1)
                s = jnp.where(col > qrow, NEG_BIG, s)
            m_prev = m_scr[u, rows, :]
            m_new = jnp.maximum(m_prev, jnp.max(s, axis=1, keepdims=True))
            alpha = jnp.exp2(m_prev - m_new)
            p = jnp.exp2(s - jnp.concatenate([m_new] * (ncols // 128), axis=1))
            pv = jnp.dot(p.astype(BF16), vext[0:ncols, :], preferred_element_type=F32)
            acc_scr[u, rows, :] = jnp.concatenate([alpha, alpha], axis=1) * acc_scr[u, rows, :] + pv
            m_scr[u, rows, :] = m_new

    def finish(u):
        o1 = acc_scr[u, 0:tq, 0:dv] / acc_scr[u, 0:tq, dv:2 * dv]
        o2 = acc_scr[u, tq:2 * tq, 0:dv] / acc_scr[u, tq:2 * tq, dv:2 * dv]
        lam_p = lam_ref[...]
        lam = (jnp.exp(jnp.sum(lam_p[0:1, :] * lam_p[1:2, :], axis=1, keepdims=True))
               - jnp.exp(jnp.sum(lam_p[2:3, :] * lam_p[3:4, :], axis=1, keepdims=True)) + lam_init)
        o = o1 - lam * o2
        r = lax.rsqrt(jnp.mean(o * o, axis=-1, keepdims=True) + EPS)
        o_ref[u * tq:(u + 1) * tq, :] = ((o * r) * subw_ref[...] * (1.0 - lam_init)).astype(o_ref.dtype)

    prepare(0)
    scores(0, 0, 0)
    for u in range(n_sub):
        g = gi * n_sub + u

        def body(i, carry, u=u):
            j = 2 * i
            scores(u, j + 1, 1)
            softmax_pv(u, j, 0, all_visible)
            scores(u, j + 2, 0)
            softmax_pv(u, j + 1, 1, all_visible)
            return carry

        lax.fori_loop(0, (nk // 2) * g, body, 0)
        for t in range(nk):
            if t + 1 < nk:
                scores(u, nk * g + t + 1, (t + 1) % 2, first_row=(t + 1) * tk)
            elif u + 1 < n_sub:
                prepare(u + 1)
                scores(u + 1, 0, 0)
            softmax_pv(u, nk * g + t, t % 2, diagonal_view(t))
        finish(u)


def _attn(P, lam_p, subw, batch, seq, tq, tk, n_sub, lam_init):
    T = P.shape[0]
    assert tq % (2 * tk) == 0
    tg = tq * n_sub
    ng = seq // tg
    q_col = J_DA_Q * (SEG // 128)
    k_col = J_DA_K * (SEG // 128)
    v_col = J_DA_V * (SEG // 128)
    return pl.pallas_call(
        functools.partial(_attn_kernel, lam_init=lam_init, strip=min(128, tq), n_sub=n_sub, tk=tk),
        grid=(batch, DA_HEADS, ng),
        in_specs=[
            pl.BlockSpec((tg, 128), lambda b, h, gi: (b * ng + gi, q_col + h)),
            pl.BlockSpec((seq, 128), lambda b, h, gi: (b, k_col + h)),
            pl.BlockSpec((seq, 128), lambda b, h, gi: (b, v_col + h)),
            pl.BlockSpec((4, DA_HEAD_DIM), lambda b, h, gi: (0, 0)),
            pl.BlockSpec((1, DA_V_HEAD_DIM), lambda b, h, gi: (0, 0)),
        ],
        out_specs=pl.BlockSpec((tg, 128), lambda b, h, gi: (b * ng + gi, h)),
        out_shape=jax.ShapeDtypeStruct((T, DA_HEADS * DA_V_HEAD_DIM), BF16),
        scratch_shapes=[
            pltpu.VMEM((n_sub, 2 * tq, 128), BF16),
            pltpu.VMEM((2, 2 * tq, tk), F32),
            pltpu.VMEM((n_sub, 2 * tq, 128), F32),
            pltpu.VMEM((n_sub, 2 * tq, 2 * DA_V_HEAD_DIM), F32),
        ],
        compiler_params=_cparams(("arbitrary", "arbitrary", "arbitrary")),
        name="attn",
    )(P, P, P, lam_p, subw)


def _ret_kernel(q_ref, k_ref, va_ref, vb_ref, ga_ref, gb_ref, dec_ref, xi_ref, zeta_ref, cd_ref, rw_ref,
                o_ref, st_scr):
    n = pl.program_id(1)
    ts = q_ref.shape[0]
    C = RET_CHUNK
    dk, dv = RET_KEY_DIM, RET_VAL_DIM
    heads_per_tile = SEG // dv

    @pl.when(n == 0)
    def _():
        st_scr[...] = jnp.zeros(st_scr.shape, F32)

    for c in range(ts // C):
        rows = slice(c * C, (c + 1) * C)
        for h in range(RET_HEADS):
            kcols = slice(h * dk, (h + 1) * dk)
            v_ref, g_ref = (va_ref, ga_ref) if h < heads_per_tile else (vb_ref, gb_ref)
            vcols = slice((h % heads_per_tile) * dv, (h % heads_per_tile + 1) * dv)
            qc = q_ref[rows, kcols]
            kc = k_ref[rows, kcols]
            vc = v_ref[rows, vcols]
            inner = lax.dot_general(qc, kc, (((1,), (1,)), ((), ())), preferred_element_type=F32) * dec_ref[h]
            st = st_scr[h]
            qx = (qc.astype(F32) * xi_ref[h]).astype(BF16)
            o = (jnp.dot(inner.astype(BF16), vc, preferred_element_type=F32)
                 + jnp.dot(qx, st.astype(BF16), preferred_element_type=F32))
            kz = (kc.astype(F32) * zeta_ref[h]).astype(BF16)
            st_scr[h] = cd_ref[h] * st + lax.dot_general(kz, vc, (((0,), (0,)), ((), ())),
                                                         preferred_element_type=F32)
            r = lax.rsqrt(jnp.mean(o * o, axis=-1, keepdims=True) + EPS)
            o_ref[rows, h * dv:(h + 1) * dv] = (
                g_ref[rows, vcols].astype(F32) * ((o * r) * rw_ref[...])).astype(o_ref.dtype)


def _ret(P, decay, xi, zeta, cd, rw, batch, seq, ts):
    T = P.shape[0]
    ns = seq // ts
    C = RET_CHUNK
    tile = lambda j: pl.BlockSpec((ts, SEG), lambda b, n: (b * ns + n, j))
    full = lambda shape: pl.BlockSpec(shape, lambda b, n: (0,) * len(shape))
    return pl.pallas_call(
        _ret_kernel,
        grid=(batch, ns),
        in_specs=[
            tile(J_RET_Q), tile(J_RET_K), tile(5), tile(6), tile(J_RET_G0), tile(J_RET_G1),
            full((RET_HEADS, C, C)),
            full((RET_HEADS, C, 1)),
            full((RET_HEADS, C, 1)),
            full((RET_HEADS, 1, 1)),
            full((1, RET_VAL_DIM)),
        ],
        out_specs=pl.BlockSpec((ts, RET_HEADS * RET_VAL_DIM), lambda b, n: (b * ns + n, 0)),
        out_shape=jax.ShapeDtypeStruct((T, RET_HEADS * RET_VAL_DIM), BF16),
        scratch_shapes=[pltpu.VMEM((RET_HEADS, RET_KEY_DIM, RET_VAL_DIM), F32)],
        compiler_params=_cparams(("arbitrary", "arbitrary")),
        name="ret",
    )(P, P, P, P, P, P, decay, xi, zeta, cd, rw)


def _merge_kernel(x_ref, oa_ref, rb_ref, ga_ref, gb_ref, wa_ref, wb_ref, wo_ref, mnw_ref,
                  wrh_ref, wrl_ref, br_ref, ltri_ref,
                  x1_ref, hm_ref, ri_ref, rw_ref, cnt_ref, carry_scr):
    i = pl.program_id(0)
    tm = x_ref.shape[0]

    @pl.when(i == 0)
    def _():
        carry_scr[...] = jnp.zeros(carry_scr.shape, F32)

    ya = jnp.dot(oa_ref[...], wa_ref[...], preferred_element_type=F32)
    yb = jnp.dot(rb_ref[...], wb_ref[...], preferred_element_type=F32)
    z = ga_ref[...].astype(F32) * ya + gb_ref[...].astype(F32) * yb
    x1 = x_ref[...] + jnp.dot(z.astype(BF16), wo_ref[...], preferred_element_type=F32)
    x1_ref[...] = x1
    r = lax.rsqrt(jnp.mean(x1 * x1, axis=-1, keepdims=True) + EPS)
    hm = (x1 * r) * mnw_ref[...]
    hm_ref[...] = _pack_bf16_pair(hm[:, 0:D_MODEL // 2], hm[:, D_MODEL // 2:D_MODEL])

    hm_hi = hm.astype(BF16)
    hm_lo = (hm - hm_hi.astype(F32)).astype(BF16)
    logits = (jnp.dot(hm_hi, wrh_ref[...], preferred_element_type=F32)
              + jnp.dot(hm_lo, wrh_ref[...], preferred_element_type=F32)
              + jnp.dot(hm_hi, wrl_ref[...], preferred_element_type=F32)) + br_ref[...]

    lane = lax.broadcasted_iota(I32, (tm, ROUTE_LANES), 1)
    lane_f = lane.astype(F32)

    def first_argmax(vals, vmax):
        return jnp.min(jnp.where(vals == vmax, lane_f, float(ROUTE_LANES)), axis=1, keepdims=True).astype(I32)

    gmask = lane < N_GROUPS
    lg = jnp.where(gmask, logits, NEG_BIG)
    mg = jnp.max(lg, axis=1, keepdims=True)
    g_idx = first_argmax(lg, mg)
    p_g = 1.0 / jnp.sum(jnp.where(gmask, jnp.exp(lg - mg), 0.0), axis=1, keepdims=True)

    lo_lane = N_GROUPS + EXPERTS_PER_GROUP * g_idx
    emask = (lane >= lo_lane) & (lane < lo_lane + EXPERTS_PER_GROUP)
    le = jnp.where(emask, logits, NEG_BIG)
    m1 = jnp.max(le, axis=1, keepdims=True)
    i1 = first_argmax(le, m1)
    le2 = jnp.where(lane == i1, NEG_BIG, le)
    m2 = jnp.max(le2, axis=1, keepdims=True)
    i2 = first_argmax(le2, m2)
    e21 = jnp.exp(m2 - m1)
    w_top1 = p_g / (1.0 + e21)
    w_top2 = p_g * e21 / (1.0 + e21)

    a1 = i1 - lo_lane
    a2 = i2 - lo_lane
    first_is_lo = a1 < a2
    a_lo = jnp.where(first_is_lo, a1, a2)
    a_hi = jnp.where(first_is_lo, a2, a1)
    w_lo = jnp.where(first_is_lo, w_top1, w_top2)
    w_hi = jnp.where(first_is_lo, w_top2, w_top1)
    pair = ((a_lo * (2 * EXPERTS_PER_GROUP - 1 - a_lo)) >> 1) + (a_hi - a_lo - 1)
    bucket = g_idx * PAIRS_PER_GROUP + pair

    carry = carry_scr[...]
    oh = jnp.where(lane == bucket, 1.0, 0.0)
    prefix = jnp.dot(ltri_ref[...], oh.astype(BF16), preferred_element_type=F32) + carry
    rank = jnp.sum(jnp.where(lane == bucket, prefix, 0.0), axis=1, keepdims=True).astype(I32)
    carry = carry + jnp.sum(oh, axis=0, keepdims=True)
    carry_scr[...] = carry
    cnt_ref[...] = carry

    zero_i = jnp.zeros((tm, ROUTE_LANES), I32)
    ri_ref[...] = jnp.where(lane == 0, (bucket << RANK_BITS) | rank, zero_i)
    rw_ref[...] = jnp.where(lane == 0, w_lo, jnp.where(lane == 1, w_hi, 0.0))


def _merge(x2, oa, rb, P, wa, wb, wo, mnw, wr_hi, wr_lo, br, ltri, tm):
    T = x2.shape[0]
    full = lambda shape: pl.BlockSpec(shape, lambda i: (0,) * len(shape))
    return pl.pallas_call(
        _merge_kernel,
        grid=(T // tm,),
        in_specs=[
            pl.BlockSpec((tm, D_MODEL), lambda i: (i, 0)),
            pl.BlockSpec((tm, D_MODEL), lambda i: (i, 0)),
            pl.BlockSpec((tm, 2 * D_MODEL), lambda i: (i, 0)),
            pl.BlockSpec((tm, SEG), lambda i: (i, J_GATE_A)),
            pl.BlockSpec((tm, SEG), lambda i: (i, J_GATE_B)),
            full((D_MODEL, D_MODEL)),
            full((2 * D_MODEL, D_MODEL)),
            full((D_MODEL, D_MODEL)),
            full((1, D_MODEL)),
            full((D_MODEL, ROUTE_LANES)),
            full((D_MODEL, ROUTE_LANES)),
            full((1, ROUTE_LANES)),
            full(ltri.shape),
        ],
        out_specs=[
            pl.BlockSpec((tm, D_MODEL), lambda i: (i, 0)),
            pl.BlockSpec((tm, D_MODEL // 2), lambda i: (i, 0)),
            pl.BlockSpec((tm, ROUTE_LANES), lambda i: (i, 0)),
            pl.BlockSpec((tm, ROUTE_LANES), lambda i: (i, 0)),
            full((1, ROUTE_LANES)),
        ],
        out_shape=[
            jax.ShapeDtypeStruct((T, D_MODEL), F32),
            jax.ShapeDtypeStruct((T, D_MODEL // 2), U32),
            jax.ShapeDtypeStruct((T, ROUTE_LANES), I32),
            jax.ShapeDtypeStruct((T, ROUTE_LANES), F32),
            jax.ShapeDtypeStruct((1, ROUTE_LANES), F32),
        ],
        scratch_shapes=[pltpu.VMEM((1, ROUTE_LANES), F32)],
        compiler_params=_cparams(("arbitrary",)),
        name="merge",
    )(x2, oa, rb, P, P, wa, wb, wo, mnw, wr_hi, wr_lo, br, ltri)


def _sorted_row(route_ref, seg_ref, t):
    word = route_ref[t]
    return seg_ref[word >> RANK_BITS] + (word & ((1 << RANK_BITS) - 1))


def _dispatch_kernel(route_ref, seg_ref, hm_ref, xs_in_ref, xs_ref, sem):
    del xs_in_ref
    i = pl.program_id(0)
    tm = hm_ref.shape[0]
    base = i * tm

    def row_copy(r):
        d = _sorted_row(route_ref, seg_ref, base + r)
        return pltpu.make_async_copy(hm_ref.at[pl.ds(r, 1), :], xs_ref.at[pl.ds(d, 1), :], sem)

    for r in range(tm):
        row_copy(r).start()
    for r in range(tm):
        row_copy(r).wait()


def _dispatch(route, seg_start, hm, xs_zero, tm):
    T = hm.shape[0]
    return pl.pallas_call(
        _dispatch_kernel,
        grid_spec=pltpu.PrefetchScalarGridSpec(
            num_scalar_prefetch=2,
            grid=(T // tm,),
            in_specs=[
                pl.BlockSpec((tm, hm.shape[1]), lambda i, route, seg: (i, 0)),
                pl.BlockSpec(memory_space=pl.ANY),
            ],
            out_specs=pl.BlockSpec(memory_space=pl.ANY),
            scratch_shapes=[pltpu.SemaphoreType.DMA(())],
        ),
        out_shape=jax.ShapeDtypeStruct(xs_zero.shape, xs_zero.dtype),
        input_output_aliases={3: 0},
        compiler_params=_cparams(("arbitrary",)),
        name="dispatch",
    )(route, seg_start, hm, xs_zero)


def _experts_kernel(tlo_ref, thi_ref, tvalid_ref, x_ref, w1l_ref, w3l_ref, w2l_ref,
                    w1h_ref, w3h_ref, w2h_ref, o_ref):
    del tlo_ref, thi_ref
    i = pl.program_id(0)

    @pl.when(tvalid_ref[i] == 1)
    def _():
        x_lo, x_hi = _unpack_bf16_pair(x_ref[...])
        x = jnp.concatenate([x_lo, x_hi], axis=1).astype(BF16)

        def mlp(w1_ref, w3_ref, w2_ref):
            a = jnp.dot(x, w1_ref[0], preferred_element_type=F32)
            b = jnp.dot(x, w3_ref[0], preferred_element_type=F32)
            h = (a * jax.nn.sigmoid(a)) * b
            return jnp.dot(h.astype(BF16), w2_ref[0], preferred_element_type=F32)

        o_ref[...] = _pack_bf16_pair(mlp(w1l_ref, w3l_ref, w2l_ref), mlp(w1h_ref, w3h_ref, w2h_ref))

    @pl.when(tvalid_ref[i] == 0)
    def _():
        o_ref[...] = jnp.zeros(o_ref.shape, o_ref.dtype)


def _experts(tile_lo, tile_hi, tile_valid, xs, w1, w3, w2, tr):
    R = xs.shape[0]
    w_up = lambda which: pl.BlockSpec(
        (1, D_MODEL, D_FF), lambda i, tlo, thi, tv: ((tlo if which == 0 else thi)[i], 0, 0))
    w_dn = lambda which: pl.BlockSpec(
        (1, D_FF, D_MODEL), lambda i, tlo, thi, tv: ((tlo if which == 0 else thi)[i], 0, 0))
    return pl.pallas_call(
        _experts_kernel,
        grid_spec=pltpu.PrefetchScalarGridSpec(
            num_scalar_prefetch=3,
            grid=(R // tr,),
            in_specs=[
                pl.BlockSpec((tr, D_MODEL // 2), lambda i, tlo, thi, tv: (i, 0)),
                w_up(0), w_up(0), w_dn(0), w_up(1), w_up(1), w_dn(1),
            ],
            out_specs=pl.BlockSpec((tr, D_MODEL), lambda i, tlo, thi, tv: (i, 0)),
        ),
        out_shape=jax.ShapeDtypeStruct((R, D_MODEL), U32),
        compiler_params=_cparams(("arbitrary",)),
        name="experts",
    )(tile_lo, tile_hi, tile_valid, xs, w1, w3, w2, w1, w3, w2)


def _combine_kernel(route_ref, seg_ref, x1_ref, rw_ref, ys_ref, o_ref, buf, sem):
    i = pl.program_id(0)
    n_blocks = pl.num_programs(0) - 1
    tm = x1_ref.shape[0]

    def row_copy(block, slot, r):
        d = _sorted_row(route_ref, seg_ref, block * tm + r)
        return pltpu.make_async_copy(ys_ref.at[pl.ds(d, 1), :], buf.at[slot, pl.ds(r, 1), :], sem.at[slot])

    @pl.when(i < n_blocks)
    def _():
        for r in range(tm):
            row_copy(i, i % 2, r).start()

    @pl.when(i > 0)
    def _():
        slot = (i - 1) % 2
        for r in range(tm):
            row_copy(i - 1, slot, r).wait()
        rw = rw_ref[...]
        y_lo, y_hi = _unpack_bf16_pair(buf[slot])
        o_ref[...] = x1_ref[...] + rw[:, 0:1] * y_lo + rw[:, 1:2] * y_hi


def _combine(route, seg_start, x1, rw, ys, tm):
    T = x1.shape[0]
    return pl.pallas_call(
        _combine_kernel,
        grid_spec=pltpu.PrefetchScalarGridSpec(
            num_scalar_prefetch=2,
            grid=(T // tm + 1,),
            in_specs=[
                pl.BlockSpec((tm, D_MODEL), lambda i, route, seg: (jnp.maximum(i - 1, 0), 0)),
                pl.BlockSpec((tm, ROUTE_LANES), lambda i, route, seg: (jnp.maximum(i - 1, 0), 0)),
                pl.BlockSpec(memory_space=pl.ANY),
            ],
            out_specs=pl.BlockSpec((tm, D_MODEL), lambda i, route, seg: (jnp.maximum(i - 1, 0), 0)),
            scratch_shapes=[pltpu.VMEM((2, tm, D_MODEL), U32), pltpu.SemaphoreType.DMA((2,))],
        ),
        out_shape=jax.ShapeDtypeStruct((T, D_MODEL), F32),
        compiler_params=_cparams(("arbitrary",)),
        name="combine",
    )(route, seg_start, x1, rw, ys)


def _rotation_tables(seq):
    angle = 1.0 / (RET_ROT_BASE ** jnp.linspace(0.0, 1.0, RET_KEY_DIM // 2, dtype=F32))
    angle = jnp.repeat(angle, 2)
    ang = jnp.arange(seq).astype(F32)[:, None] * angle[None, :]
    sin = jnp.sin(ang)
    even = (jnp.arange(RET_KEY_DIM) % 2 == 0)[None, :]
    return jnp.cos(ang), jnp.where(even, -sin, 0.0), jnp.where(even, 0.0, sin)


def _retention_tables():
    C = RET_CHUNK
    log_g = jnp.log1p(-jnp.exp2(-5.0 - jnp.arange(RET_HEADS, dtype=F32)))
    j = jnp.arange(C, dtype=F32)
    rel = j[:, None] - j[None, :]
    decay = jnp.where(rel >= 0, jnp.exp(log_g[:, None, None] * jnp.maximum(rel, 0.0)), 0.0)
    xi = jnp.exp(log_g[:, None] * (j + 1.0))[:, :, None]
    zeta = jnp.exp(log_g[:, None] * (C - 1.0 - j))[:, :, None]
    cd = jnp.exp(log_g * C)[:, None, None]
    return decay, xi, zeta, cd


def _pair_tables():
    lo, hi = [], []
    for g in range(N_GROUPS):
        for a in range(EXPERTS_PER_GROUP):
            for b in range(a + 1, EXPERTS_PER_GROUP):
                lo.append(g * EXPERTS_PER_GROUP + a)
                hi.append(g * EXPERTS_PER_GROUP + b)
    return np.asarray(lo, np.int32), np.asarray(hi, np.int32)


def _pick(pref, n):
    t = min(pref, n)
    while n % t:
        t //= 2
    return t


def kernel(x, attn_norm_w, w_in, q_norm_w, k_norm_w, lambda_q1, lambda_k1, lambda_q2, lambda_k2,
           da_subln_w, ret_norm_w, w_branch_a, w_branch_b, w_out, moe_norm_w,
           w_group_router, b_group_router, w_expert_router, b_expert_router, w1, w3, w2):
    B, S, D = x.shape
    T = B * S
    assert T <= (1 << RANK_BITS)
    depth = attn_norm_w.shape[0]

    tm_proj = _pick(2048, S)
    tq = _pick(2048, S)
    tk = min(512, tq // 2)
    n_sub = 1
    ts_ret = _pick(512, S)
    tm_merge = _pick(512, T)
    tm_row = _pick(512, T)
    tr = 256

    cos_t, sine_t, sino_t = _rotation_tables(S)
    decay, xi, zeta, cd = _retention_tables()
    gidx = np.arange(256) // DA_HEAD_DIM
    gsum = jnp.asarray((gidx[:, None] == gidx[None, :]).astype(np.float32) / DA_HEAD_DIM, BF16)
    ltri = jnp.asarray(np.tril(np.ones((tm_merge, tm_merge), np.float32), -1), BF16)
    pair_lo, pair_hi = _pair_tables()
    n_tiles = T // tr + N_BUCKETS
    R = n_tiles * tr

    x2 = x.reshape(T, D)
    for l in range(depth):
        lam_init = 0.8 - 0.6 * math.exp(-0.3 * l)
        P = _proj(x2, attn_norm_w[l][None, :], w_in[l].astype(BF16), cos_t, sine_t, sino_t,
                  jnp.tile(q_norm_w[l], SEG // DA_HEAD_DIM)[None, :],
                  jnp.tile(k_norm_w[l], SEG // DA_HEAD_DIM)[None, :], gsum, S, tm_proj)
        lam_p = jnp.stack([lambda_q1[l], lambda_k1[l], lambda_q2[l], lambda_k2[l]]).astype(F32)
        oa = _attn(P, lam_p, da_subln_w[l][None, :], B, S, tq, tk, n_sub, lam_init)
        rb = _ret(P, decay, xi, zeta, cd, ret_norm_w[l][None, :].astype(F32), B, S, ts_ret)

        w_router = jnp.zeros((D, ROUTE_LANES), F32)
        w_router = w_router.at[:, 0:N_GROUPS].set(w_group_router[l])
        w_router = w_router.at[:, N_GROUPS:N_GROUPS + N_EXPERTS].set(w_expert_router[l])
        wr_hi = w_router.astype(BF16)
        wr_lo = (w_router - wr_hi.astype(F32)).astype(BF16)
        b_router = jnp.zeros((1, ROUTE_LANES), F32)
        b_router = b_router.at[0, 0:N_GROUPS].set(b_group_router[l])
        b_router = b_router.at[0, N_GROUPS:N_GROUPS + N_EXPERTS].set(b_expert_router[l].reshape(-1))

        x1, hm, ri, rw, cnt = _merge(
            x2, oa, rb, P, w_branch_a[l].astype(BF16), w_branch_b[l].astype(BF16),
            w_out[l].astype(BF16), moe_norm_w[l][None, :], wr_hi, wr_lo, b_router, ltri, tm_merge)

        counts = cnt[0, 0:N_BUCKETS].astype(I32)
        tiles_per = (counts + tr - 1) // tr
        tile_end = jnp.cumsum(tiles_per)
        seg_start = (tile_end - tiles_per) * tr
        route = ri[:, 0]
        tile_ids = jnp.arange(n_tiles, dtype=I32)
        n_used = tile_end[-1]
        tile_bucket = jnp.sum((tile_end[None, :] <= jnp.minimum(tile_ids, n_used - 1)[:, None]).astype(I32), axis=1)
        tile_bucket = jnp.minimum(tile_bucket, N_BUCKETS - 1)
        tile_lo = jnp.asarray(pair_lo)[tile_bucket]
        tile_hi = jnp.asarray(pair_hi)[tile_bucket]
        tile_valid = (tile_ids < n_used).astype(I32)

        xs = _dispatch(route, seg_start, hm, jnp.zeros((R, D // 2), U32), tm_row)
        ys = _experts(tile_lo, tile_hi, tile_valid, xs, w1[l].astype(BF16), w3[l].astype(BF16),
                      w2[l].astype(BF16), tr)
        x2 = _combine(route, seg_start, x1, rw, ys, tm_row)
    return x2.reshape(B, S, D)
```

```python
import functools
import math

import numpy as np
import jax
import jax.numpy as jnp
from jax import lax
from jax.experimental import pallas as pl
from jax.experimental.pallas import tpu as pltpu

F32 = jnp.float32
BF16 = jnp.bfloat16
I32 = jnp.int32

D_MODEL = 1024
EPS = 1e-6

DA_HEADS = 8
DA_HEAD_DIM = 64
DA_V_HEAD_DIM = 128

RET_HEADS = 4
RET_KEY_DIM = 256
RET_VAL_DIM = 512
RET_CHUNK = 256
RET_ROT_BASE = 10000.0

SEG = 1024
IN_WIDTH = 11 * SEG
J_DA_Q, J_DA_K, J_DA_V, J_RET_Q, J_RET_K = 0, 1, 2, 3, 4
J_RET_G0, J_RET_G1, J_GATE_A, J_GATE_B = 7, 8, 9, 10

N_GROUPS = 4
EXPERTS_PER_GROUP = 8
N_EXPERTS = 32
D_FF = 512
PAIRS_PER_GROUP = 28
N_BUCKETS = N_GROUPS * PAIRS_PER_GROUP
ROUTE_LANES = 128
RANK_BITS = 17

NEG_BIG = -1e30
LOG2E = 1.4426950408889634

VMEM_LIMIT = 56 * 1024 * 1024


def _cparams(sem):
    return pltpu.CompilerParams(dimension_semantics=sem, vmem_limit_bytes=VMEM_LIMIT)


def _proj_kernel(x_ref, nw_ref, w_ref, cos_ref, sine_ref, sino_ref, qnw_ref, knw_ref, gsum_ref, o_ref, h_scr,
                 *, strip):
    j = pl.program_id(1)
    tm = x_ref.shape[0]

    @pl.when(j == 0)
    def _():
        x = x_ref[...]
        r = lax.rsqrt(jnp.mean(x * x, axis=-1, keepdims=True) + EPS)
        h_scr[...] = ((x * r) * nw_ref[...]).astype(BF16)

    def run(epilogue):
        for s in range(tm // strip):
            rows = slice(s * strip, (s + 1) * strip)
            y = jnp.dot(h_scr[rows, :], w_ref[...], preferred_element_type=F32)
            epilogue(y, rows)

    def qk_norm(w_ref_, scale):
        def epilogue(y, rows):
            for c in range(SEG // 256):
                sl = slice(c * 256, (c + 1) * 256)
                yc = y[:, sl]
                ms = jnp.dot((yc * yc).astype(BF16), gsum_ref[...], preferred_element_type=F32)
                o_ref[rows, sl] = ((yc * lax.rsqrt(ms + EPS)) * (w_ref_[:, sl] * scale)).astype(o_ref.dtype)
        return epilogue

    def rotate(scale):
        def epilogue(y, rows):
            for c in range(SEG // 128):
                sl = slice(c * 128, (c + 1) * 128)
                tl = slice((c % 2) * 128, (c % 2 + 1) * 128)
                t = y[:, sl]
                t_next = pltpu.roll(t, 127, 1)
                t_prev = pltpu.roll(t, 1, 1)
                out = t * cos_ref[rows, tl] + t_next * sine_ref[rows, tl] + t_prev * sino_ref[rows, tl]
                o_ref[rows, sl] = (out * scale).astype(o_ref.dtype)
        return epilogue

    def pointwise(fn):
        def epilogue(y, rows):
            o_ref[rows, :] = fn(y).astype(o_ref.dtype)
        return epilogue

    @pl.when(j == J_DA_Q)
    def _():
        run(qk_norm(qnw_ref, DA_HEAD_DIM ** -0.5 * LOG2E))

    @pl.when(j == J_DA_K)
    def _():
        run(qk_norm(knw_ref, 1.0))

    @pl.when(j == J_RET_Q)
    def _():
        run(rotate(1.0))

    @pl.when(j == J_RET_K)
    def _():
        run(rotate(RET_KEY_DIM ** -0.5))

    @pl.when((j == J_DA_V) | (j == 5) | (j == 6))
    def _():
        run(pointwise(lambda y: y))

    @pl.when((j == J_RET_G0) | (j == J_RET_G1))
    def _():
        run(pointwise(lambda y: y * jax.nn.sigmoid(y)))

    @pl.when((j == J_GATE_A) | (j == J_GATE_B))
    def _():
        run(pointwise(jax.nn.sigmoid))


def _proj(x2, attn_norm_w, w_in_bf, cos_t, sine_t, sino_t, qnw_t, knw_t, gsum, seq, tm):
    T = x2.shape[0]
    n_pos_blocks = seq // tm
    table = pl.BlockSpec((tm, RET_KEY_DIM),
                         lambda i, j: ((i + jnp.where(j > J_RET_K, 1, 0)) % n_pos_blocks, 0))
    n_row_blocks = T // tm
    return pl.pallas_call(
        functools.partial(_proj_kernel, strip=max(tm // 4, min(256, tm))),
        grid=(n_row_blocks, IN_WIDTH // SEG),
        in_specs=[
            pl.BlockSpec((tm, D_MODEL),
                         lambda i, j: (jnp.minimum(i + jnp.where(j > 0, 1, 0), n_row_blocks - 1), 0)),
            pl.BlockSpec((1, D_MODEL), lambda i, j: (0, 0)),
            pl.BlockSpec((D_MODEL, SEG), lambda i, j: (0, j)),
            table, table, table,
            pl.BlockSpec((1, SEG), lambda i, j: (0, 0)),
            pl.BlockSpec((1, SEG), lambda i, j: (0, 0)),
            pl.BlockSpec((256, 256), lambda i, j: (0, 0)),
        ],
        out_specs=pl.BlockSpec((tm, SEG), lambda i, j: (i, j)),
        out_shape=jax.ShapeDtypeStruct((T, IN_WIDTH), BF16),
        scratch_shapes=[pltpu.VMEM((tm, D_MODEL), BF16)],
        compiler_params=_cparams(("arbitrary", "arbitrary")),
        name="proj",
    )(x2, attn_norm_w, w_in_bf, cos_t, sine_t, sino_t, qnw_t, knw_t, gsum)


def _attn_kernel(q_ref, k_ref, v_ref, lam_ref, subw_ref, o_ref, q2_scr, s_scr, m_scr, acc_scr, *,
                 lam_init, strip, n_sub, tk):
    gi = pl.program_id(2)
    tq = q_ref.shape[0] // n_sub
    nk = tq // tk
    dv = DA_V_HEAD_DIM
    n_strips = 2 * tq // strip
    ones = jnp.ones((tk, dv), BF16)

    def prepare(u):
        q = q_ref[u * tq:(u + 1) * tq, :]
        lane = lax.broadcasted_iota(I32, q.shape, 1)
        zero = jnp.zeros_like(q)
        q2_scr[u, 0:tq, :] = jnp.where(lane < DA_HEAD_DIM, q, zero)
        q2_scr[u, tq:2 * tq, :] = jnp.where(lane >= DA_HEAD_DIM, q, zero)
        m_scr[u] = jnp.full(m_scr.shape[1:], NEG_BIG, F32)
        acc_scr[u] = jnp.zeros(acc_scr.shape[1:], F32)

    def scores(u, j, slot, first_row=0):
        off = pl.multiple_of(j * tk, tk)
        k = k_ref[pl.ds(off, tk), :]
        spans = [(first_row, tq), (tq + first_row, 2 * tq)] if first_row else [(0, 2 * tq)]
        for a, b in spans:
            s_scr[slot, a:b, :] = lax.dot_general(q2_scr[u, a:b, :], k, (((1,), (1,)), ((), ())),
                                                  preferred_element_type=F32)

    VISIBLE, HIDDEN = "visible", "hidden"

    def all_visible(r):
        return VISIBLE

    def diagonal_view(t):
        def view(r):
            q0 = (r * strip) % tq - t * tk
            return HIDDEN if q0 < 0 else (q0 if q0 < tk else VISIBLE)
        return view

    def softmax_pv(u, j, slot, strip_view):
        off = pl.multiple_of(j * tk, tk)
        vext = jnp.concatenate([v_ref[pl.ds(off, tk), :], ones], axis=1)
        for r in range(n_strips):
            view = strip_view(r)
            if view == HIDDEN:
                continue
            rows = slice(r * strip, (r + 1) * strip)
            if view == VISIBLE:
                ncols = tk
                s = s_scr[slot, rows, :]
            else:
                ncols = min(tk, view + strip)
                s = s_scr[slot, rows, 0:ncols]
                qrow = view + lax.broadcasted_iota(I32, (strip, ncols), 0)
                col = lax.broadcasted_iota(I32, (strip, ncols), 1)
                s = jnp.where(col > qrow, NEG_BIG, s)
            m_prev = m_scr[u, rows, :]
            m_new = jnp.maximum(m_prev, jnp.max(s, axis=1, keepdims=True))
            alpha = jnp.exp2(m_prev - m_new)
            p = jnp.exp2(s - jnp.concatenate([m_new] * (ncols // 128), axis=1))
            pv = jnp.dot(p.astype(BF16), vext[0:ncols, :], preferred_element_type=F32)
            acc_scr[u, rows, :] = jnp.concatenate([alpha, alpha], axis=1) * acc_scr[u, rows, :] + pv
            m_scr[u, rows, :] = m_new

    def finish(u):
        o1 = acc_scr[u, 0:tq, 0:dv] / acc_scr[u, 0:tq, dv:2 * dv]
        o2 = acc_scr[u, tq:2 * tq, 0:dv] / acc_scr[u, tq:2 * tq, dv:2 * dv]
        lam_p = lam_ref[...]
        lam = (jnp.exp(jnp.sum(lam_p[0:1, :] * lam_p[1:2, :], axis=1, keepdims=True))
               - jnp.exp(jnp.sum(lam_p[2:3, :] * lam_p[3:4, :], axis=1, keepdims=True)) + lam_init)
        o = o1 - lam * o2
        r = lax.rsqrt(jnp.mean(o * o, axis=-1, keepdims=True) + EPS)
        o_ref[u * tq:(u + 1) * tq, :] = ((o * r) * subw_ref[...] * (1.0 - lam_init)).astype(o_ref.dtype)

    prepare(0)
    scores(0, 0, 0)
    for u in range(n_sub):
        g = gi * n_sub + u

        def body(i, carry, u=u):
            j = 2 * i
            scores(u, j + 1, 1)
            softmax_pv(u, j, 0, all_visible)
            scores(u, j + 2, 0)
            softmax_pv(u, j + 1, 1, all_visible)
            return carry

        lax.fori_loop(0, (nk // 2) * g, body, 0)
        for t in range(nk):
            if t + 1 < nk:
                scores(u, nk * g + t + 1, (t + 1) % 2, first_row=(t + 1) * tk)
            elif u + 1 < n_sub:
                prepare(u + 1)
                scores(u + 1, 0, 0)
            softmax_pv(u, nk * g + t, t % 2, diagonal_view(t))
        finish(u)


def _attn(P, lam_p, subw, batch, seq, tq, tk, n_sub, lam_init):
    T = P.shape[0]
    assert tq % (2 * tk) == 0
    tg = tq * n_sub
    ng = seq // tg
    q_col = J_DA_Q * (SEG // 128)
    k_col = J_DA_K * (SEG // 128)
    v_col = J_DA_V * (SEG // 128)
    return pl.pallas_call(
        functools.partial(_attn_kernel, lam_init=lam_init, strip=min(128, tq), n_sub=n_sub, tk=tk),
        grid=(batch, DA_HEADS, ng),
        in_specs=[
            pl.BlockSpec((tg, 128), lambda b, h, gi: (b * ng + gi, q_col + h)),
            pl.BlockSpec((seq, 128), lambda b, h, gi: (b, k_col + h)),
            pl.BlockSpec((seq, 128), lambda b, h, gi: (b, v_col + h)),
            pl.BlockSpec((4, DA_HEAD_DIM), lambda b, h, gi: (0, 0)),
            pl.BlockSpec((1, DA_V_HEAD_DIM), lambda b, h, gi: (0, 0)),
        ],
        out_specs=pl.BlockSpec((tg, 128), lambda b, h, gi: (b * ng + gi, h)),
        out_shape=jax.ShapeDtypeStruct((T, DA_HEADS * DA_V_HEAD_DIM), BF16),
        scratch_shapes=[
            pltpu.VMEM((n_sub, 2 * tq, 128), BF16),
            pltpu.VMEM((2, 2 * tq, tk), F32),
            pltpu.VMEM((n_sub, 2 * tq, 128), F32),
            pltpu.VMEM((n_sub, 2 * tq, 2 * DA_V_HEAD_DIM), F32),
        ],
        compiler_params=_cparams(("arbitrary", "arbitrary", "arbitrary")),
        name="attn",
    )(P, P, P, lam_p, subw)


def _ret_kernel(q_ref, k_ref, va_ref, vb_ref, ga_ref, gb_ref, dec_ref, xi_ref, zeta_ref, cd_ref, rw_ref,
                o_ref, st_scr):
    n = pl.program_id(1)
    ts = q_ref.shape[0]
    C = RET_CHUNK
    dk, dv = RET_KEY_DIM, RET_VAL_DIM
    heads_per_tile = SEG // dv

    @pl.when(n == 0)
    def _():
        st_scr[...] = jnp.zeros(st_scr.shape, F32)

    for c in range(ts // C):
        rows = slice(c * C, (c + 1) * C)
        for h in range(RET_HEADS):
            kcols = slice(h * dk, (h + 1) * dk)
            v_ref, g_ref = (va_ref, ga_ref) if h < heads_per_tile else (vb_ref, gb_ref)
            vcols = slice((h % heads_per_tile) * dv, (h % heads_per_tile + 1) * dv)
            qc = q_ref[rows, kcols]
            kc = k_ref[rows, kcols]
            vc = v_ref[rows, vcols]
            inner = lax.dot_general(qc, kc, (((1,), (1,)), ((), ())), preferred_element_type=F32) * dec_ref[h]
            st = st_scr[h]
            qx = (qc.astype(F32) * xi_ref[h]).astype(BF16)
            o = (jnp.dot(inner.astype(BF16), vc, preferred_element_type=F32)
                 + jnp.dot(qx, st.astype(BF16), preferred_element_type=F32))
            kz = (kc.astype(F32) * zeta_ref[h]).astype(BF16)
            st_scr[h] = cd_ref[h] * st + lax.dot_general(kz, vc, (((0,), (0,)), ((), ())),
                                                         preferred_element_type=F32)
            r = lax.rsqrt(jnp.mean(o * o, axis=-1, keepdims=True) + EPS)
            o_ref[rows, h * dv:(h + 1) * dv] = (
                g_ref[rows, vcols].astype(F32) * ((o * r) * rw_ref[...])).astype(o_ref.dtype)


def _ret(P, decay, xi, zeta, cd, rw, batch, seq, ts):
    T = P.shape[0]
    ns = seq // ts
    C = RET_CHUNK
    tile = lambda j: pl.BlockSpec((ts, SEG), lambda b, n: (b * ns + n, j))
    full = lambda shape: pl.BlockSpec(shape, lambda b, n: (0,) * len(shape))
    return pl.pallas_call(
        _ret_kernel,
        grid=(batch, ns),
        in_specs=[
            tile(J_RET_Q), tile(J_RET_K), tile(5), tile(6), tile(J_RET_G0), tile(J_RET_G1),
            full((RET_HEADS, C, C)),
            full((RET_HEADS, C, 1)),
            full((RET_HEADS, C, 1)),
            full((RET_HEADS, 1, 1)),
            full((1, RET_VAL_DIM)),
        ],
        out_specs=pl.BlockSpec((ts, RET_HEADS * RET_VAL_DIM), lambda b, n: (b * ns + n, 0)),
        out_shape=jax.ShapeDtypeStruct((T, RET_HEADS * RET_VAL_DIM), BF16),
        scratch_shapes=[pltpu.VMEM((RET_HEADS, RET_KEY_DIM, RET_VAL_DIM), F32)],
        compiler_params=_cparams(("arbitrary", "arbitrary")),
        name="ret",
    )(P, P, P, P, P, P, decay, xi, zeta, cd, rw)


def _merge_kernel(x_ref, oa_ref, rb_ref, ga_ref, gb_ref, wa_ref, wb_ref, wo_ref, mnw_ref,
                  wrh_ref, wrl_ref, br_ref, ltri_ref,
                  x1_ref, hm_ref, ri_ref, rw_ref, cnt_ref, carry_scr):
    i = pl.program_id(0)
    tm = x_ref.shape[0]

    @pl.when(i == 0)
    def _():
        carry_scr[...] = jnp.zeros(carry_scr.shape, F32)

    ya = jnp.dot(oa_ref[...], wa_ref[...], preferred_element_type=F32)
    yb = jnp.dot(rb_ref[...], wb_ref[...], preferred_element_type=F32)
    z = ga_ref[...].astype(F32) * ya + gb_ref[...].astype(F32) * yb
    x1 = x_ref[...] + jnp.dot(z.astype(BF16), wo_ref[...], preferred_element_type=F32)
    x1_ref[...] = x1
    r = lax.rsqrt(jnp.mean(x1 * x1, axis=-1, keepdims=True) + EPS)
    hm = (x1 * r) * mnw_ref[...]
    hm_ref[...] = hm

    hm_hi = hm.astype(BF16)
    hm_lo = (hm - hm_hi.astype(F32)).astype(BF16)
    logits = (jnp.dot(hm_hi, wrh_ref[...], preferred_element_type=F32)
              + jnp.dot(hm_lo, wrh_ref[...], preferred_element_type=F32)
              + jnp.dot(hm_hi, wrl_ref[...], preferred_element_type=F32)) + br_ref[...]

    lane = lax.broadcasted_iota(I32, (tm, ROUTE_LANES), 1)
    lane_f = lane.astype(F32)

    def first_argmax(vals, vmax):
        return jnp.min(jnp.where(vals == vmax, lane_f, float(ROUTE_LANES)), axis=1, keepdims=True).astype(I32)

    gmask = lane < N_GROUPS
    lg = jnp.where(gmask, logits, NEG_BIG)
    mg = jnp.max(lg, axis=1, keepdims=True)
    g_idx = first_argmax(lg, mg)
    p_g = 1.0 / jnp.sum(jnp.where(gmask, jnp.exp(lg - mg), 0.0), axis=1, keepdims=True)

    lo_lane = N_GROUPS + EXPERTS_PER_GROUP * g_idx
    emask = (lane >= lo_lane) & (lane < lo_lane + EXPERTS_PER_GROUP)
    le = jnp.where(emask, logits, NEG_BIG)
    m1 = jnp.max(le, axis=1, keepdims=True)
    i1 = first_argmax(le, m1)
    le2 = jnp.where(lane == i1, NEG_BIG, le)
    m2 = jnp.max(le2, axis=1, keepdims=True)
    i2 = first_argmax(le2, m2)
    e21 = jnp.exp(m2 - m1)
    w_top1 = p_g / (1.0 + e21)
    w_top2 = p_g * e21 / (1.0 + e21)

    a1 = i1 - lo_lane
    a2 = i2 - lo_lane
    first_is_lo = a1 < a2
    a_lo = jnp.where(first_is_lo, a1, a2)
    a_hi = jnp.where(first_is_lo, a2, a1)
    w_lo = jnp.where(first_is_lo, w_top1, w_top2)
    w_hi = jnp.where(first_is_lo, w_top2, w_top1)
    pair = ((a_lo * (2 * EXPERTS_PER_GROUP - 1 - a_lo)) >> 1) + (a_hi - a_lo - 1)
    bucket = g_idx * PAIRS_PER_GROUP + pair

    carry = carry_scr[...]
    oh = jnp.where(lane == bucket, 1.0, 0.0)
    prefix = jnp.dot(ltri_ref[...], oh.astype(BF16), preferred_element_type=F32) + carry
    rank = jnp.sum(jnp.where(lane == bucket, prefix, 0.0), axis=1, keepdims=True).astype(I32)
    carry = carry + jnp.sum(oh, axis=0, keepdims=True)
    carry_scr[...] = carry
    cnt_ref[...] = carry

    zero_i = jnp.zeros((tm, ROUTE_LANES), I32)
    ri_ref[...] = jnp.where(lane == 0, (bucket << RANK_BITS) | rank, zero_i)
    rw_ref[...] = jnp.where(lane == 0, w_lo, jnp.where(lane == 1, w_hi, 0.0))


def _merge(x2, oa, rb, P, wa, wb, wo, mnw, wr_hi, wr_lo, br, ltri, tm):
    T = x2.shape[0]
    full = lambda shape: pl.BlockSpec(shape, lambda i: (0,) * len(shape))
    return pl.pallas_call(
        _merge_kernel,
        grid=(T // tm,),
        in_specs=[
            pl.BlockSpec((tm, D_MODEL), lambda i: (i, 0)),
            pl.BlockSpec((tm, D_MODEL), lambda i: (i, 0)),
            pl.BlockSpec((tm, 2 * D_MODEL), lambda i: (i, 0)),
            pl.BlockSpec((tm, SEG), lambda i: (i, J_GATE_A)),
            pl.BlockSpec((tm, SEG), lambda i: (i, J_GATE_B)),
            full((D_MODEL, D_MODEL)),
            full((2 * D_MODEL, D_MODEL)),
            full((D_MODEL, D_MODEL)),
            full((1, D_MODEL)),
            full((D_MODEL, ROUTE_LANES)),
            full((D_MODEL, ROUTE_LANES)),
            full((1, ROUTE_LANES)),
            full(ltri.shape),
        ],
        out_specs=[
            pl.BlockSpec((tm, D_MODEL), lambda i: (i, 0)),
            pl.BlockSpec((tm, D_MODEL), lambda i: (i, 0)),
            pl.BlockSpec((tm, ROUTE_LANES), lambda i: (i, 0)),
            pl.BlockSpec((tm, ROUTE_LANES), lambda i: (i, 0)),
            full((1, ROUTE_LANES)),
        ],
        out_shape=[
            jax.ShapeDtypeStruct((T, D_MODEL), F32),
            jax.ShapeDtypeStruct((T, D_MODEL), F32),
            jax.ShapeDtypeStruct((T, ROUTE_LANES), I32),
            jax.ShapeDtypeStruct((T, ROUTE_LANES), F32),
            jax.ShapeDtypeStruct((1, ROUTE_LANES), F32),
        ],
        scratch_shapes=[pltpu.VMEM((1, ROUTE_LANES), F32)],
        compiler_params=_cparams(("arbitrary",)),
        name="merge",
    )(x2, oa, rb, P, P, wa, wb, wo, mnw, wr_hi, wr_lo, br, ltri)


def _sorted_row(route_ref, seg_ref, t):
    word = route_ref[t]
    return seg_ref[word >> RANK_BITS] + (word & ((1 << RANK_BITS) - 1))


def _dispatch_kernel(route_ref, seg_ref, hm_ref, xs_in_ref, xs_ref, sem):
    del xs_in_ref
    i = pl.program_id(0)
    tm = hm_ref.shape[0]
    base = i * tm

    def row_copy(r):
        d = _sorted_row(route_ref, seg_ref, base + r)
        return pltpu.make_async_copy(hm_ref.at[pl.ds(r, 1), :], xs_ref.at[pl.ds(d, 1), :], sem)

    for r in range(tm):
        row_copy(r).start()
    for r in range(tm):
        row_copy(r).wait()


def _dispatch(route, seg_start, hm, xs_zero, tm):
    T = hm.shape[0]
    return pl.pallas_call(
        _dispatch_kernel,
        grid_spec=pltpu.PrefetchScalarGridSpec(
            num_scalar_prefetch=2,
            grid=(T // tm,),
            in_specs=[
                pl.BlockSpec((tm, hm.shape[1]), lambda i, route, seg: (i, 0)),
                pl.BlockSpec(memory_space=pl.ANY),
            ],
            out_specs=pl.BlockSpec(memory_space=pl.ANY),
            scratch_shapes=[pltpu.SemaphoreType.DMA(())],
        ),
        out_shape=jax.ShapeDtypeStruct(xs_zero.shape, xs_zero.dtype),
        input_output_aliases={3: 0},
        compiler_params=_cparams(("arbitrary",)),
        name="dispatch",
    )(route, seg_start, hm, xs_zero)


def _experts_kernel(tlo_ref, thi_ref, tvalid_ref, x_ref, w1l_ref, w3l_ref, w2l_ref,
                    w1h_ref, w3h_ref, w2h_ref, o_ref):
    del tlo_ref, thi_ref
    i = pl.program_id(0)

    @pl.when(tvalid_ref[i] == 1)
    def _():
        x = x_ref[...].astype(BF16)

        def mlp(w1_ref, w3_ref, w2_ref):
            a = jnp.dot(x, w1_ref[0], preferred_element_type=F32)
            b = jnp.dot(x, w3_ref[0], preferred_element_type=F32)
            h = (a * jax.nn.sigmoid(a)) * b
            return jnp.dot(h.astype(BF16), w2_ref[0], preferred_element_type=F32)

        o_ref[:, 0:D_MODEL] = mlp(w1l_ref, w3l_ref, w2l_ref)
        o_ref[:, D_MODEL:2 * D_MODEL] = mlp(w1h_ref, w3h_ref, w2h_ref)

    @pl.when(tvalid_ref[i] == 0)
    def _():
        o_ref[...] = jnp.zeros(o_ref.shape, o_ref.dtype)


def _experts(tile_lo, tile_hi, tile_valid, xs, w1, w3, w2, tr):
    R = xs.shape[0]
    w_up = lambda which: pl.BlockSpec(
        (1, D_MODEL, D_FF), lambda i, tlo, thi, tv: ((tlo if which == 0 else thi)[i], 0, 0))
    w_dn = lambda which: pl.BlockSpec(
        (1, D_FF, D_MODEL), lambda i, tlo, thi, tv: ((tlo if which == 0 else thi)[i], 0, 0))
    return pl.pallas_call(
        _experts_kernel,
        grid_spec=pltpu.PrefetchScalarGridSpec(
            num_scalar_prefetch=3,
            grid=(R // tr,),
            in_specs=[
                pl.BlockSpec((tr, D_MODEL), lambda i, tlo, thi, tv: (i, 0)),
                w_up(0), w_up(0), w_dn(0), w_up(1), w_up(1), w_dn(1),
            ],
            out_specs=pl.BlockSpec((tr, 2 * D_MODEL), lambda i, tlo, thi, tv: (i, 0)),
        ),
        out_shape=jax.ShapeDtypeStruct((R, 2 * D_MODEL), F32),
        compiler_params=_cparams(("arbitrary",)),
        name="experts",
    )(tile_lo, tile_hi, tile_valid, xs, w1, w3, w2, w1, w3, w2)


def _combine_kernel(route_ref, seg_ref, x1_ref, rw_ref, ys_ref, o_ref, buf, sem):
    i = pl.program_id(0)
    n_blocks = pl.num_programs(0) - 1
    tm = x1_ref.shape[0]

    def row_copy(block, slot, r):
        d = _sorted_row(route_ref, seg_ref, block * tm + r)
        return pltpu.make_async_copy(ys_ref.at[pl.ds(d, 1), :], buf.at[slot, pl.ds(r, 1), :], sem.at[slot])

    @pl.when(i < n_blocks)
    def _():
        for r in range(tm):
            row_copy(i, i % 2, r).start()

    @pl.when(i > 0)
    def _():
        slot = (i - 1) % 2
        for r in range(tm):
            row_copy(i - 1, slot, r).wait()
        rw = rw_ref[...]
        o_ref[...] = (x1_ref[...] + rw[:, 0:1] * buf[slot, :, 0:D_MODEL]
                      + rw[:, 1:2] * buf[slot, :, D_MODEL:2 * D_MODEL])


def _combine(route, seg_start, x1, rw, ys, tm):
    T = x1.shape[0]
    return pl.pallas_call(
        _combine_kernel,
        grid_spec=pltpu.PrefetchScalarGridSpec(
            num_scalar_prefetch=2,
            grid=(T // tm + 1,),
            in_specs=[
                pl.BlockSpec((tm, D_MODEL), lambda i, route, seg: (jnp.maximum(i - 1, 0), 0)),
                pl.BlockSpec((tm, ROUTE_LANES), lambda i, route, seg: (jnp.maximum(i - 1, 0), 0)),
                pl.BlockSpec(memory_space=pl.ANY),
            ],
            out_specs=pl.BlockSpec((tm, D_MODEL), lambda i, route, seg: (jnp.maximum(i - 1, 0), 0)),
            scratch_shapes=[pltpu.VMEM((2, tm, 2 * D_MODEL), F32), pltpu.SemaphoreType.DMA((2,))],
        ),
        out_shape=jax.ShapeDtypeStruct((T, D_MODEL), F32),
        compiler_params=_cparams(("arbitrary",)),
        name="combine",
    )(route, seg_start, x1, rw, ys)


def _rotation_tables(seq):
    angle = 1.0 / (RET_ROT_BASE ** jnp.linspace(0.0, 1.0, RET_KEY_DIM // 2, dtype=F32))
    angle = jnp.repeat(angle, 2)
    ang = jnp.arange(seq).astype(F32)[:, None] * angle[None, :]
    sin = jnp.sin(ang)
    even = (jnp.arange(RET_KEY_DIM) % 2 == 0)[None, :]
    return jnp.cos(ang), jnp.where(even, -sin, 0.0), jnp.where(even, 0.0, sin)


def _retention_tables():
    C = RET_CHUNK
    log_g = jnp.log1p(-jnp.exp2(-5.0 - jnp.arange(RET_HEADS, dtype=F32)))
    j = jnp.arange(C, dtype=F32)
    rel = j[:, None] - j[None, :]
    decay = jnp.where(rel >= 0, jnp.exp(log_g[:, None, None] * jnp.maximum(rel, 0.0)), 0.0)
    xi = jnp.exp(log_g[:, None] * (j + 1.0))[:, :, None]
    zeta = jnp.exp(log_g[:, None] * (C - 1.0 - j))[:, :, None]
    cd = jnp.exp(log_g * C)[:, None, None]
    return decay, xi, zeta, cd


def _pair_tables():
    lo, hi = [], []
    for g in range(N_GROUPS):
        for a in range(EXPERTS_PER_GROUP):
            for b in range(a + 1, EXPERTS_PER_GROUP):
                lo.append(g * EXPERTS_PER_GROUP + a)
                hi.append(g * EXPERTS_PER_GROUP + b)
    return np.asarray(lo, np.int32), np.asarray(hi, np.int32)


def _pick(pref, n):
    t = min(pref, n)
    while n % t:
        t //= 2
    return t


def kernel(x, attn_norm_w, w_in, q_norm_w, k_norm_w, lambda_q1, lambda_k1, lambda_q2, lambda_k2,
           da_subln_w, ret_norm_w, w_branch_a, w_branch_b, w_out, moe_norm_w,
           w_group_router, b_group_router, w_expert_router, b_expert_router, w1, w3, w2):
    B, S, D = x.shape
    T = B * S
    assert T <= (1 << RANK_BITS)
    depth = attn_norm_w.shape[0]

    tm_proj = _pick(2048, S)
    tq = _pick(2048, S)
    tk = min(512, tq // 2)
    n_sub = 1
    ts_ret = _pick(1024, S)
    tm_merge = _pick(512, T)
    tm_row = _pick(1024, T)
    tr = 256

    cos_t, sine_t, sino_t = _rotation_tables(S)
    decay, xi, zeta, cd = _retention_tables()
    gidx = np.arange(256) // DA_HEAD_DIM
    gsum = jnp.asarray((gidx[:, None] == gidx[None, :]).astype(np.float32) / DA_HEAD_DIM, BF16)
    ltri = jnp.asarray(np.tril(np.ones((tm_merge, tm_merge), np.float32), -1), BF16)
    pair_lo, pair_hi = _pair_tables()
    n_tiles = T // tr + N_BUCKETS
    R = n_tiles * tr

    x2 = x.reshape(T, D)
    for l in range(depth):
        lam_init = 0.8 - 0.6 * math.exp(-0.3 * l)
        P = _proj(x2, attn_norm_w[l][None, :], w_in[l].astype(BF16), cos_t, sine_t, sino_t,
                  jnp.tile(q_norm_w[l], SEG // DA_HEAD_DIM)[None, :],
                  jnp.tile(k_norm_w[l], SEG // DA_HEAD_DIM)[None, :], gsum, S, tm_proj)
        lam_p = jnp.stack([lambda_q1[l], lambda_k1[l], lambda_q2[l], lambda_k2[l]]).astype(F32)
        oa = _attn(P, lam_p, da_subln_w[l][None, :], B, S, tq, tk, n_sub, lam_init)
        rb = _ret(P, decay, xi, zeta, cd, ret_norm_w[l][None, :].astype(F32), B, S, ts_ret)

        w_router = jnp.zeros((D, ROUTE_LANES), F32)
        w_router = w_router.at[:, 0:N_GROUPS].set(w_group_router[l])
        w_router = w_router.at[:, N_GROUPS:N_GROUPS + N_EXPERTS].set(w_expert_router[l])
        wr_hi = w_router.astype(BF16)
        wr_lo = (w_router - wr_hi.astype(F32)).astype(BF16)
        b_router = jnp.zeros((1, ROUTE_LANES), F32)
        b_router = b_router.at[0, 0:N_GROUPS].set(b_group_router[l])
        b_router = b_router.at[0, N_GROUPS:N_GROUPS + N_EXPERTS].set(b_expert_router[l].reshape(-1))

        x1, hm, ri, rw, cnt = _merge(
            x2, oa, rb, P, w_branch_a[l].astype(BF16), w_branch_b[l].astype(BF16),
            w_out[l].astype(BF16), moe_norm_w[l][None, :], wr_hi, wr_lo, b_router, ltri, tm_merge)

        counts = cnt[0, 0:N_BUCKETS].astype(I32)
        tiles_per = (counts + tr - 1) // tr
        tile_end = jnp.cumsum(tiles_per)
        seg_start = (tile_end - tiles_per) * tr
        route = ri[:, 0]
        tile_ids = jnp.arange(n_tiles, dtype=I32)
        n_used = tile_end[-1]
        tile_bucket = jnp.sum((tile_end[None, :] <= jnp.minimum(tile_ids, n_used - 1)[:, None]).astype(I32), axis=1)
        tile_bucket = jnp.minimum(tile_bucket, N_BUCKETS - 1)
        tile_lo = jnp.asarray(pair_lo)[tile_bucket]
        tile_hi = jnp.asarray(pair_hi)[tile_bucket]
        tile_valid = (tile_ids < n_used).astype(I32)

        xs = _dispatch(route, seg_start, hm, jnp.zeros((R, D), F32), tm_row)
        ys = _experts(tile_lo, tile_hi, tile_valid, xs, w1[l].astype(BF16), w3[l].astype(BF16),
                      w2[l].astype(BF16), tr)
        x2 = _combine(route, seg_start, x1, rw, ys, tm_row)
    return x2.reshape(B, S, D)
```
